```python
import jax, jax.numpy as jnp
from jax import lax
import numpy as np

D_MODEL = 4096
BATCH = 2
SEQ = 8192
DEPTH = 2

GRID_W = 64
CTX_LEN = 256
N_MIXERS = 2
N_HEADS = 32
N_KV_HEADS = 8
HEAD_DIM = D_MODEL // N_HEADS
GQA_GROUP = N_HEADS // N_KV_HEADS
Q_DIM = N_HEADS * HEAD_DIM
KV_DIM = N_KV_HEADS * HEAD_DIM
Q_BLOCK = 128
ROPE_THETA = 10000.0
N_FREQ = HEAD_DIM // 4
ATTN_SCALE = HEAD_DIM ** -0.5
CONV_WIDTH = 31
CONV_PAD = CONV_WIDTH // 2
N_GROUPS = 4
EXPERTS_PER_GROUP = 8
N_EXPERTS = N_GROUPS * EXPERTS_PER_GROUP
TOP_K = 2
D_EXPERT = D_MODEL // 8
MOE_BLOCK = 256
LN_EPS = 1e-5
QK_EPS = 1e-6
ALPHA = (2 * DEPTH) ** 0.25
BETA = (8 * DEPTH) ** -0.25
N_ATTN_LAYERS = (DEPTH + N_MIXERS - 1) // N_MIXERS
N_CONV_LAYERS = DEPTH // N_MIXERS

kernel_name = "hybrid_attn_conformer_hmoe_diffusion"


def _layer_norm(x, g, b):
    xf = x.astype(jnp.float32)
    mu = jnp.mean(xf, axis=-1, keepdims=True)
    var = jnp.mean(jnp.square(xf - mu), axis=-1, keepdims=True)
    return ((xf - mu) * lax.rsqrt(var + LN_EPS) * g.astype(jnp.float32) + b.astype(jnp.float32)).astype(x.dtype)


def _rms_norm(x, g):
    xf = x.astype(jnp.float32)
    ms = jnp.mean(jnp.square(xf), axis=-1, keepdims=True)
    return (xf * lax.rsqrt(ms + QK_EPS) * g.astype(jnp.float32)).astype(x.dtype)


def _modulation(cond, w, b):
    m = jax.nn.silu(cond) @ w + b
    return jnp.split(m, 6, axis=-1)


def _axial_angles(n_rows):
    inv = ROPE_THETA ** (-jnp.arange(N_FREQ, dtype=jnp.float32) / N_FREQ)
    row = jnp.repeat(jnp.arange(n_rows, dtype=jnp.float32), GRID_W)
    col = jnp.tile(jnp.arange(GRID_W, dtype=jnp.float32), n_rows)
    return row[:, None] * inv, col[:, None] * inv


def _rotate(seg, ang):
    half = seg.shape[-1] // 2
    x1, x2 = seg[..., :half], seg[..., half:]
    cos = jnp.cos(ang)[None, :, None, :]
    sin = jnp.sin(ang)[None, :, None, :]
    return jnp.concatenate([x1 * cos - x2 * sin, x2 * cos + x1 * sin], axis=-1)


def _axial_rope(t, ang_r, ang_c):
    tf = t.astype(jnp.float32)
    a = HEAD_DIM // 2
    return jnp.concatenate([_rotate(tf[..., :a], ang_r), _rotate(tf[..., a:], ang_c)], axis=-1).astype(t.dtype)


def _project_qkv(t, w_qkv, q_gain, k_gain, with_q):
    B, L, _ = t.shape
    y = t @ (w_qkv if with_q else w_qkv[:, Q_DIM:])
    q = None
    if with_q:
        q = _rms_norm(y[..., :Q_DIM].reshape(B, L, N_HEADS, HEAD_DIM), q_gain)
        y = y[..., Q_DIM:]
    k = _rms_norm(y[..., :KV_DIM].reshape(B, L, N_KV_HEADS, HEAD_DIM), k_gain)
    v = y[..., KV_DIM:].reshape(B, L, N_KV_HEADS, HEAD_DIM)
    return q, k, v


def _gqa_softmax(q, k, v):
    s = jnp.einsum('bqkgd,bskd->bkgqs', q, k, preferred_element_type=jnp.float32) * ATTN_SCALE
    p = jax.nn.softmax(s, axis=-1).astype(v.dtype)
    return jnp.einsum('bkgqs,bskd->bqkgd', p, v)


def _attention_mixer(h, hc, w_qkv, q_gain, k_gain, w_o, ang_r, ang_c, ctx_out):
    B, S, D = h.shape
    C = hc.shape[1]
    q, k, v = _project_qkv(h, w_qkv, q_gain, k_gain, True)
    q = _axial_rope(q, ang_r, ang_c)
    k = _axial_rope(k, ang_r, ang_c)
    qc, kc, vc = _project_qkv(hc, w_qkv, q_gain, k_gain, ctx_out)
    k_all = jnp.concatenate([k, kc], axis=1)
    v_all = jnp.concatenate([v, vc], axis=1)
    nb = S // Q_BLOCK
    q_blocks = q.reshape(B, nb, Q_BLOCK, N_KV_HEADS, GQA_GROUP, HEAD_DIM).transpose(1, 0, 2, 3, 4, 5)
    o = lax.map(lambda qb: _gqa_softmax(qb, k_all, v_all), q_blocks)
    o = o.transpose(1, 0, 2, 3, 4, 5).reshape(B, S, D)
    y = o @ w_o
    yc = None
    if ctx_out:
        oc = _gqa_softmax(qc.reshape(B, C, N_KV_HEADS, GQA_GROUP, HEAD_DIM), kc, vc).reshape(B, C, D)
        yc = oc @ w_o
    return y, yc


def _conformer_conv(h, w_pw1, b_pw1, w_dw, b_dw, n_g, n_b, w_pw2):
    a, gate = jnp.split(h @ w_pw1 + b_pw1, 2, axis=-1)
    u = a * jax.nn.sigmoid(gate)
    u = lax.conv_general_dilated(u, w_dw[:, None, :], (1,), ((CONV_PAD, CONV_PAD),),
                                 dimension_numbers=('NWC', 'WIO', 'NWC'),
                                 feature_group_count=u.shape[-1]) + b_dw
    u = jax.nn.silu(_layer_norm(u, n_g, n_b))
    return u @ w_pw2


def _hier_moe(h, w_rg, b_rg, w_re, b_re, w_gate, w_up, w_down):
    N, D = h.shape
    gl = (h @ w_rg).astype(jnp.float32) + b_rg
    gidx = jnp.argmax(gl, axis=-1).astype(jnp.int32)
    gw = jnp.take_along_axis(jax.nn.softmax(gl, axis=-1), gidx[:, None], axis=-1)
    el = (h @ w_re).astype(jnp.float32) + b_re
    el = el.reshape(N, N_GROUPS, EXPERTS_PER_GROUP)
    el = jnp.take_along_axis(el, gidx[:, None, None], axis=1)[:, 0]
    top_l, top_i = lax.top_k(el, TOP_K)
    ew = jax.nn.softmax(top_l, axis=-1) * gw
    eid = gidx[:, None] * EXPERTS_PER_GROUP + top_i.astype(jnp.int32)
    A = N * TOP_K
    eid_f = eid.reshape(A)
    w_f = ew.reshape(A)
    tok_f = jnp.repeat(jnp.arange(N, dtype=jnp.int32), TOP_K)
    order = jnp.argsort(eid_f)
    se, stok, sw = eid_f[order], tok_f[order], w_f[order]
    counts = jnp.bincount(eid_f, length=N_EXPERTS).astype(jnp.int32)
    pcounts = (counts + MOE_BLOCK - 1) // MOE_BLOCK * MOE_BLOCK
    starts = jnp.cumsum(counts) - counts
    pends = jnp.cumsum(pcounts)
    pstarts = pends - pcounts
    dest = pstarts[se] + jnp.arange(A, dtype=jnp.int32) - starts[se]
    n_blocks = -(-A // MOE_BLOCK) + N_EXPERTS
    P = n_blocks * MOE_BLOCK
    buf_tok = jnp.full((P,), N, dtype=jnp.int32).at[dest].set(stok)
    buf_w = jnp.zeros((P,), h.dtype).at[dest].set(sw.astype(h.dtype))
    block_e = jnp.minimum(jnp.searchsorted(pends, jnp.arange(n_blocks, dtype=jnp.int32) * MOE_BLOCK,
                                           side='right'), N_EXPERTS - 1).astype(jnp.int32)
    h_pad = jnp.concatenate([h, jnp.zeros((1, D), h.dtype)], axis=0)

    def step(acc, blk):
        rows, wts, e = blk
        xb = h_pad[rows]
        yb = (jax.nn.silu(xb @ w_gate[e]) * (xb @ w_up[e])) @ w_down[e]
        return acc.at[rows].add(yb * wts[:, None]), None

    acc, _ = lax.scan(step, jnp.zeros((N + 1, D), h.dtype),
                      (buf_tok.reshape(n_blocks, MOE_BLOCK), buf_w.reshape(n_blocks, MOE_BLOCK), block_e))
    return acc[:N]


def setup_inputs(seed: int = 0) -> dict:
    key = jax.random.key(seed)
    ks = jax.random.split(key, 32)
    nrm = lambda k, shape, s: jax.random.normal(k, shape, jnp.float32) * s
    D = D_MODEL
    return {
        "x": nrm(ks[0], (BATCH, SEQ, D), 1.0),
        "c": nrm(ks[1], (BATCH, D), 1.0),
        "ctx": nrm(ks[2], (BATCH, CTX_LEN, D), 1.0),
        "c_ctx": nrm(ks[3], (D,), 1.0),
        "w_ada": nrm(ks[4], (DEPTH, D, 6 * D), 0.5 * D ** -0.5),
        "b_ada": nrm(ks[5], (DEPTH, 6 * D), 0.02),
        "ln_g": 1.0 + nrm(ks[6], (DEPTH, 2, D), 0.02),
        "ln_b": nrm(ks[7], (DEPTH, 2, D), 0.02),
        "w_qkv": nrm(ks[8], (N_ATTN_LAYERS, D, Q_DIM + 2 * KV_DIM), D ** -0.5),
        "q_gain": 1.0 + nrm(ks[9], (N_ATTN_LAYERS, HEAD_DIM), 0.02),
        "k_gain": 1.0 + nrm(ks[10], (N_ATTN_LAYERS, HEAD_DIM), 0.02),
        "w_o": nrm(ks[11], (N_ATTN_LAYERS, Q_DIM, D), BETA * Q_DIM ** -0.5),
        "w_pw1": nrm(ks[12], (N_CONV_LAYERS, D, 2 * D), D ** -0.5),
        "b_pw1": nrm(ks[13], (N_CONV_LAYERS, 2 * D), 0.02),
        "w_dw": nrm(ks[14], (N_CONV_LAYERS, CONV_WIDTH, D), CONV_WIDTH ** -0.5),
        "b_dw": nrm(ks[15], (N_CONV_LAYERS, D), 0.02),
        "conv_ln_g": 1.0 + nrm(ks[16], (N_CONV_LAYERS, D), 0.02),
        "conv_ln_b": nrm(ks[17], (N_CONV_LAYERS, D), 0.02),
        "w_pw2": nrm(ks[18], (N_CONV_LAYERS, D, D), BETA * D ** -0.5),
        "w_rg": nrm(ks[19], (DEPTH, D, N_GROUPS), D ** -0.5),
        "b_rg": nrm(ks[20], (DEPTH, N_GROUPS), 0.01),
        "w_re": nrm(ks[21], (DEPTH, D, N_EXPERTS), D ** -0.5),
        "b_re": nrm(ks[22], (DEPTH, N_EXPERTS), 0.01),
        "w_gate": nrm(ks[23], (DEPTH, N_EXPERTS, D, D_EXPERT), D ** -0.5),
        "w_up": nrm(ks[24], (DEPTH, N_EXPERTS, D, D_EXPERT), D ** -0.5),
        "w_down": nrm(ks[25], (DEPTH, N_EXPERTS, D_EXPERT, D), BETA * D_EXPERT ** -0.5),
    }


def reference(x, c, ctx, c_ctx, w_ada, b_ada, ln_g, ln_b, w_qkv, q_gain, k_gain, w_o,
              w_pw1, b_pw1, w_dw, b_dw, conv_ln_g, conv_ln_b, w_pw2,
              w_rg, b_rg, w_re, b_re, w_gate, w_up, w_down):
    B, S, D = x.shape
    C = ctx.shape[1]
    n_rows = S // GRID_W
    ang_r, ang_c = _axial_angles(n_rows)
    last_ctx_layer = ((DEPTH - 1) // N_MIXERS) * N_MIXERS
    for i in range(DEPTH):
        mixer = i % N_MIXERS
        j = i // N_MIXERS
        need_ctx = i < last_ctx_layer
        sh1, sc1, g1, sh2, sc2, g2 = [m[:, None, :] for m in _modulation(c, w_ada[i], b_ada[i])]
        if mixer == 0 or need_ctx:
            csh1, csc1, cg1, csh2, csc2, cg2 = _modulation(c_ctx, w_ada[i], b_ada[i])
        h = x * (1.0 + sc1) + sh1
        yc = None
        if mixer == 0:
            hc = ctx * (1.0 + csc1) + csh1
            y, yc = _attention_mixer(h, hc, w_qkv[j], q_gain[j], k_gain[j], w_o[j], ang_r, ang_c, need_ctx)
        else:
            conv_args = (w_pw1[j], b_pw1[j], w_dw[j], b_dw[j], conv_ln_g[j], conv_ln_b[j], w_pw2[j])
            y = _conformer_conv(h, *conv_args)
            if need_ctx:
                yc = _conformer_conv(ctx * (1.0 + csc1) + csh1, *conv_args)
        x = _layer_norm(ALPHA * x + g1 * y, ln_g[i, 0], ln_b[i, 0])
        moe_args = (w_rg[i], b_rg[i], w_re[i], b_re[i], w_gate[i], w_up[i], w_down[i])
        h = (x * (1.0 + sc2) + sh2).reshape(B * S, D)
        if need_ctx:
            ctx = _layer_norm(ALPHA * ctx + cg1 * yc, ln_g[i, 0], ln_b[i, 0])
            hc = (ctx * (1.0 + csc2) + csh2).reshape(B * C, D)
            f_all = _hier_moe(jnp.concatenate([h, hc], axis=0), *moe_args)
            f, fc = f_all[:B * S].reshape(B, S, D), f_all[B * S:].reshape(B, C, D)
            ctx = _layer_norm(ALPHA * ctx + cg2 * fc, ln_g[i, 1], ln_b[i, 1])
        else:
            f = _hier_moe(h, *moe_args).reshape(B, S, D)
        x = _layer_norm(ALPHA * x + g2 * f, ln_g[i, 1], ln_b[i, 1])
    return x
```

```python
import functools

import jax
import jax.numpy as jnp
from jax import lax
from jax.experimental import pallas as pl
from jax.experimental.pallas import tpu as pltpu

F32 = jnp.float32
BF16 = jnp.bfloat16

HEAD_DIM = 128
GQA_GROUP = 4
GRID_W = 64
ROPE_THETA = 10000.0
N_FREQ = HEAD_DIM // 4
ATTN_SCALE = HEAD_DIM ** -0.5
CONV_WIDTH = 31
CONV_PAD = CONV_WIDTH // 2
N_GROUPS = 4
EXPERTS_PER_GROUP = 8
N_EXPERTS = N_GROUPS * EXPERTS_PER_GROUP
TOP_K = 2
MOE_BLOCK = 256
LN_EPS = 1e-5
QK_EPS = 1e-6
DEPTH = 2
ALPHA = (2 * DEPTH) ** 0.25

LANES = 128
HALO = 16
ADA_ROWS = 8
VMEM_LIMIT = 56 * 1024 * 1024


def _params(sem, vmem=VMEM_LIMIT):
    return pltpu.CompilerParams(dimension_semantics=sem, vmem_limit_bytes=vmem)


def _sigmoid(x):
    return 1.0 / (1.0 + jnp.exp(-x))


def _dot(a, b):
    return jnp.dot(a, b, preferred_element_type=F32)


def _ada_kernel(c_ref, w_ref, b_ref, o_ref):
    c = c_ref[...]
    s = (c * _sigmoid(c)).astype(BF16)
    o_ref[...] = _dot(s, w_ref[...].astype(BF16)) + b_ref[...]


def _ada_modulation(cond, w_ada, b_ada, tn=512):
    depth, d, n6 = w_ada.shape
    return pl.pallas_call(
        _ada_kernel,
        out_shape=jax.ShapeDtypeStruct((depth, ADA_ROWS, n6), F32),
        grid=(depth, n6 // tn),
        in_specs=[
            pl.BlockSpec((ADA_ROWS, d), lambda l, j: (0, 0)),
            pl.BlockSpec((None, d, tn), lambda l, j: (l, 0, j)),
            pl.BlockSpec((None, 1, tn), lambda l, j: (l, 0, j)),
        ],
        out_specs=pl.BlockSpec((None, ADA_ROWS, tn), lambda l, j: (l, 0, j)),
        compiler_params=_params(("parallel", "parallel")),
        name="ada_modulation",
    )(cond, w_ada, b_ada.reshape(depth, 1, n6))


def _modulate_kernel(x_ref, sc_ref, sh_ref, o_ref):
    o_ref[...] = (x_ref[...] * (1.0 + sc_ref[...]) + sh_ref[...]).astype(o_ref.dtype)


def _modulate(x, sc, sh, tm=256):
    b, s, d = x.shape
    tm = min(tm, s)
    return pl.pallas_call(
        _modulate_kernel,
        out_shape=jax.ShapeDtypeStruct((b, s, d), BF16),
        grid=(b, s // tm),
        in_specs=[
            pl.BlockSpec((None, tm, d), lambda bi, i: (bi, i, 0)),
            pl.BlockSpec((None, 1, d), lambda bi, i: (bi, 0, 0)),
            pl.BlockSpec((None, 1, d), lambda bi, i: (bi, 0, 0)),
        ],
        out_specs=pl.BlockSpec((None, tm, d), lambda bi, i: (bi, i, 0)),
        compiler_params=_params(("parallel", "parallel")),
        name="modulate",
    )(x, sc, sh)


def _qkv_kernel(x_ref, w_ref, gain_ref, scale_ref, cos_ref, sina_ref, sinb_ref, o_ref, *, first_v_tile, col_off):
    j = pl.program_id(1) + col_off
    y = _dot(x_ref[...], w_ref[...])
    tn = y.shape[1]

    @pl.when(j >= first_v_tile)
    def _():
        o_ref[...] = y.astype(o_ref.dtype)

    @pl.when(j < first_v_tile)
    def _():
        cos = cos_ref[...]
        sina = sina_ref[...]
        sinb = sinb_ref[...]
        for h in range(tn // HEAD_DIM):
            sl = slice(h * HEAD_DIM, (h + 1) * HEAD_DIM)
            yh = y[:, sl]
            ms = jnp.mean(yh * yh, axis=-1, keepdims=True)
            yn = yh * lax.rsqrt(ms + QK_EPS) * gain_ref[:, sl]
            rot = (yn * cos + pltpu.roll(yn, HEAD_DIM - N_FREQ, 1) * sina
                   + pltpu.roll(yn, N_FREQ, 1) * sinb)
            o_ref[:, sl] = (rot * scale_ref[:, sl]).astype(o_ref.dtype)


def _qkv_project(x2d, w, gain, scale, cos, sina, sinb, *, q_dim, kv_dim, col_off_cols, n_cols, tm, tn=512):
    m, d = x2d.shape
    tm = min(tm, m, cos.shape[0])
    tn = min(tn, kv_dim)
    pos_tiles = cos.shape[0] // tm
    col_off = col_off_cols // tn
    kern = functools.partial(_qkv_kernel, first_v_tile=(q_dim + kv_dim) // tn, col_off=col_off)
    return pl.pallas_call(
        kern,
        out_shape=jax.ShapeDtypeStruct((m, n_cols), BF16),
        grid=(m // tm, n_cols // tn),
        in_specs=[
            pl.BlockSpec((tm, d), lambda i, j: (i, 0)),
            pl.BlockSpec((d, tn), lambda i, j: (0, j + col_off)),
            pl.BlockSpec((1, tn), lambda i, j: (0, j + col_off)),
            pl.BlockSpec((1, tn), lambda i, j: (0, j + col_off)),
            pl.BlockSpec((tm, HEAD_DIM), lambda i, j: (i % pos_tiles, 0)),
            pl.BlockSpec((tm, HEAD_DIM), lambda i, j: (i % pos_tiles, 0)),
            pl.BlockSpec((tm, HEAD_DIM), lambda i, j: (i % pos_tiles, 0)),
        ],
        out_specs=pl.BlockSpec((tm, tn), lambda i, j: (i, j)),
        compiler_params=_params(("parallel", "parallel")),
        name="qkv_project",
    )(x2d, w, gain, scale, cos, sina, sinb)


def _attn_kernel(q_ref, k_ref, v_ref, kc_ref, vc_ref, o_ref, qs_ref, m_ref, l_ref, acc_ref, *, tq, tk, n_chunks):
    for g in range(GQA_GROUP):
        qs_ref[g * tq:(g + 1) * tq, :] = q_ref[:, g * HEAD_DIM:(g + 1) * HEAD_DIM]
    m_ref[...] = jnp.full(m_ref.shape, -jnp.inf, F32)
    l_ref[...] = jnp.zeros(l_ref.shape, F32)
    acc_ref[...] = jnp.zeros(acc_ref.shape, F32)

    def update(kblk, vblk):
        s = lax.dot_general(qs_ref[...], kblk, (((1,), (1,)), ((), ())), preferred_element_type=F32)
        m_prev = m_ref[...]
        m_new = jnp.maximum(m_prev, jnp.max(s, axis=-1, keepdims=True))
        alpha = jnp.exp(m_prev - m_new)
        p = jnp.exp(s - m_new)
        l_ref[...] = alpha * l_ref[...] + jnp.sum(p, axis=-1, keepdims=True)
        acc_ref[...] = alpha * acc_ref[...] + _dot(p.astype(BF16), vblk)
        m_ref[...] = m_new

    def body(j, carry):
        st = pl.multiple_of(j * tk, tk)
        update(k_ref[pl.ds(st, tk), :], v_ref[pl.ds(st, tk), :])
        return carry

    lax.fori_loop(0, n_chunks, body, 0)
    update(kc_ref[...], vc_ref[...])
    out = acc_ref[...] / l_ref[...]
    for g in range(GQA_GROUP):
        o_ref[:, g * HEAD_DIM:(g + 1) * HEAD_DIM] = out[g * tq:(g + 1) * tq, :].astype(o_ref.dtype)


def _attention(qkv, kvc, *, n_heads, n_kv, tq=256, tk=768):
    b, s, _ = qkv.shape
    c = kvc.shape[1]
    tq = min(tq, s)
    if s % tk:
        tk = s
    gw = GQA_GROUP * HEAD_DIM
    rows = GQA_GROUP * tq
    kern = functools.partial(_attn_kernel, tq=tq, tk=tk, n_chunks=s // tk)
    return pl.pallas_call(
        kern,
        out_shape=jax.ShapeDtypeStruct((b, s, n_heads * HEAD_DIM), BF16),
        grid=(b, n_kv, s // tq),
        in_specs=[
            pl.BlockSpec((None, tq, gw), lambda bi, h, i: (bi, i, h)),
            pl.BlockSpec((None, s, HEAD_DIM), lambda bi, h, i: (bi, 0, n_heads + h)),
            pl.BlockSpec((None, s, HEAD_DIM), lambda bi, h, i: (bi, 0, n_heads + n_kv + h)),
            pl.BlockSpec((None, c, HEAD_DIM), lambda bi, h, i: (bi, 0, h)),
            pl.BlockSpec((None, c, HEAD_DIM), lambda bi, h, i: (bi, 0, n_kv + h)),
        ],
        out_specs=pl.BlockSpec((None, tq, gw), lambda bi, h, i: (bi, i, h)),
        scratch_shapes=[
            pltpu.VMEM((rows, HEAD_DIM), BF16),
            pltpu.VMEM((rows, 1), F32),
            pltpu.VMEM((rows, 1), F32),
            pltpu.VMEM((rows, HEAD_DIM), F32),
        ],
        compiler_params=_params(("parallel", "parallel", "arbitrary")),
        name="attention",
    )(qkv, qkv, qkv, kvc, kvc)


def _matmul_kernel(x_ref, w_ref, o_ref):
    o_ref[...] = _dot(x_ref[...], w_ref[...]).astype(o_ref.dtype)


def _matmul(x2d, w, tm=512, tn=1024):
    m, k = x2d.shape
    n = w.shape[1]
    tm, tn = min(tm, m), min(tn, n)
    return pl.pallas_call(
        _matmul_kernel,
        out_shape=jax.ShapeDtypeStruct((m, n), F32),
        grid=(m // tm, n // tn),
        in_specs=[pl.BlockSpec((tm, k), lambda i, j: (i, 0)), pl.BlockSpec((k, tn), lambda i, j: (0, j))],
        out_specs=pl.BlockSpec((tm, tn), lambda i, j: (i, j)),
        compiler_params=_params(("parallel", "parallel")),
        name="matmul",
    )(x2d, w)


def _glu_kernel(x_ref, wa_ref, wg_ref, ba_ref, bg_ref, o_ref):
    x = x_ref[...]
    a = _dot(x, wa_ref[...]) + ba_ref[...]
    g = _dot(x, wg_ref[...]) + bg_ref[...]
    o_ref[...] = a * _sigmoid(g)


def _glu_matmul(x2d, w, bias, tm=512, tn=512):
    m, k = x2d.shape
    n = w.shape[1] // 2
    tm, tn = min(tm, m), min(tn, n)
    nt = n // tn
    bias2 = bias.reshape(1, 2 * n)
    return pl.pallas_call(
        _glu_kernel,
        out_shape=jax.ShapeDtypeStruct((m, n), F32),
        grid=(m // tm, nt),
        in_specs=[
            pl.BlockSpec((tm, k), lambda i, j: (i, 0)),
            pl.BlockSpec((k, tn), lambda i, j: (0, j)),
            pl.BlockSpec((k, tn), lambda i, j: (0, j + nt)),
            pl.BlockSpec((1, tn), lambda i, j: (0, j)),
            pl.BlockSpec((1, tn), lambda i, j: (0, j + nt)),
        ],
        out_specs=pl.BlockSpec((tm, tn), lambda i, j: (i, j)),
        compiler_params=_params(("parallel", "parallel")),
        name="glu_matmul",
    )(x2d, w, w, bias2, bias2)


def _layer_norm_rows(v, g, b):
    mu = jnp.mean(v, axis=-1, keepdims=True)
    cen = v - mu
    var = jnp.mean(cen * cen, axis=-1, keepdims=True)
    return cen * lax.rsqrt(var + LN_EPS) * g + b


def _pack_bf16_pairs(h):
    half = h.shape[1] // 2
    hb = h.astype(BF16).astype(F32)
    lo = pltpu.bitcast(hb[:, :half], jnp.uint32)
    hi = pltpu.bitcast(hb[:, half:], jnp.uint32)
    return (lo >> 16) | (hi & jnp.uint32(0xFFFF0000))


def _unpack_bf16_pairs(w):
    lo = pltpu.bitcast(w << 16, F32).astype(BF16)
    hi = pltpu.bitcast(w & jnp.uint32(0xFFFF0000), F32).astype(BF16)
    return lo, hi


def _route(logits):
    lane = lax.broadcasted_iota(jnp.int32, logits.shape, 1).astype(F32)
    neg = -jnp.inf
    big = float(LANES)
    is_g = lane < N_GROUPS
    gl = jnp.where(is_g, logits, neg)
    gmax = jnp.max(gl, axis=-1, keepdims=True)
    gidx = jnp.min(jnp.where(gl == gmax, lane, big), axis=-1, keepdims=True)
    gsum = jnp.sum(jnp.where(is_g, jnp.exp(gl - gmax), 0.0), axis=-1, keepdims=True)
    gw = 1.0 / gsum
    lo = N_GROUPS + EXPERTS_PER_GROUP * gidx
    el = jnp.where(lane >= lo, jnp.where(lane < lo + EXPERTS_PER_GROUP, logits, neg), neg)
    t1 = jnp.max(el, axis=-1, keepdims=True)
    i1 = jnp.min(jnp.where(el == t1, lane, big), axis=-1, keepdims=True)
    el2 = jnp.where(lane == i1, neg, el)
    t2 = jnp.max(el2, axis=-1, keepdims=True)
    i2 = jnp.min(jnp.where(el2 == t2, lane, big), axis=-1, keepdims=True)
    e = jnp.exp(t2 - t1)
    w1 = gw / (1.0 + e)
    w2 = gw * e / (1.0 + e)
    out = jnp.where(lane == 0.0, i1 - N_GROUPS, 0.0)
    out = jnp.where(lane == 1.0, i2 - N_GROUPS, out)
    out = jnp.where(lane == 2.0, w1, out)
    out = jnp.where(lane == 3.0, w2, out)
    return out


def _ln_router_kernel(x_ref, y_ref, g_ref, lng_ref, lnb_ref, sc_ref, sh_ref, wr_ref, br_ref,
                      xo_ref, hp_ref, route_ref):
    v = ALPHA * x_ref[...] + g_ref[...] * y_ref[...]
    xn = _layer_norm_rows(v, lng_ref[...], lnb_ref[...])
    xo_ref[...] = xn
    h = xn * (1.0 + sc_ref[...]) + sh_ref[...]
    hp_ref[...] = _pack_bf16_pairs(h)
    h_hi = h.astype(BF16)
    h_lo = (h - h_hi.astype(F32)).astype(BF16)
    wr = wr_ref[...]
    w_hi = wr.astype(BF16)
    w_lo = (wr - w_hi.astype(F32)).astype(BF16)
    logits = _dot(h_hi, w_hi) + _dot(h_lo, w_hi) + _dot(h_hi, w_lo) + br_ref[...]
    route_ref[...] = _route(logits)


def _ln_router(x, y, gate, lng, lnb, sc, sh, wr, br, tm=256):
    b, s, d = x.shape
    tm = min(tm, s)
    row = lambda: pl.BlockSpec((None, tm, d), lambda bi, i: (bi, i, 0))
    per_b = lambda: pl.BlockSpec((None, 1, d), lambda bi, i: (bi, 0, 0))
    const = lambda shape: pl.BlockSpec(shape, lambda bi, i: (0,) * len(shape))
    return pl.pallas_call(
        _ln_router_kernel,
        out_shape=(
            jax.ShapeDtypeStruct((b, s, d), F32),
            jax.ShapeDtypeStruct((b, s, d // 2), jnp.uint32),
            jax.ShapeDtypeStruct((b, s, LANES), F32),
        ),
        grid=(b, s // tm),
        in_specs=[row(), row(), per_b(), const((1, d)), const((1, d)), per_b(), per_b(),
                  const((d, LANES)), const((1, LANES))],
        out_specs=(
            row(),
            pl.BlockSpec((None, tm, d // 2), lambda bi, i: (bi, i, 0)),
            pl.BlockSpec((None, tm, LANES), lambda bi, i: (bi, i, 0)),
        ),
        compiler_params=_params(("parallel", "parallel")),
        name="ln_router",
    )(x, y, gate, lng, lnb, sc, sh, wr, br)


def _moe_kernel(be_ref, tok_ref, nu_ref, h_hbm, wg_ref, wu_ref, wd_ref, o_ref, xb_ref, sem):
    i = pl.program_id(0)
    n_used = nu_ref[0]
    half = xb_ref.shape[2]

    def row_copy(tok, slot, r):
        return pltpu.make_async_copy(h_hbm.at[pl.ds(tok, 1)], xb_ref.at[slot, pl.ds(r, 1)], sem.at[slot])

    def issue(blk, slot):
        base = blk * MOE_BLOCK

        def body(r, carry):
            row_copy(tok_ref[base + r], slot, r).start()
            return carry

        lax.fori_loop(0, MOE_BLOCK, body, 0, unroll=8)

    def wait_all(slot):
        def body(r, carry):
            row_copy(0, slot, r).wait()
            return carry

        lax.fori_loop(0, MOE_BLOCK, body, 0, unroll=8)

    @pl.when(i == 0)
    def _():
        issue(0, 0)

    @pl.when(i + 1 < n_used)
    def _():
        issue(i + 1, (i + 1) % 2)

    @pl.when(i < n_used)
    def _():
        slot = i % 2
        wait_all(slot)
        lo, hi = _unpack_bf16_pairs(xb_ref[slot])
        gate = _dot(lo, wg_ref[:half, :]) + _dot(hi, wg_ref[half:, :])
        up = _dot(lo, wu_ref[:half, :]) + _dot(hi, wu_ref[half:, :])
        act = (gate * _sigmoid(gate) * up).astype(BF16)
        o_ref[...] = _dot(act, wd_ref[...])

    @pl.when(i >= n_used)
    def _():
        o_ref[...] = jnp.zeros(o_ref.shape, o_ref.dtype)


def _moe_experts(hp2d, block_e, buf_tok, n_used, wg, wu, wd):
    n_blocks = block_e.shape[0]
    half = hp2d.shape[1]
    d = 2 * half
    de = wg.shape[2]
    grid_spec = pltpu.PrefetchScalarGridSpec(
        num_scalar_prefetch=3,
        grid=(n_blocks,),
        in_specs=[
            pl.BlockSpec(memory_space=pl.ANY),
            pl.BlockSpec((None, d, de), lambda i, be, tok, nu: (be[i], 0, 0)),
            pl.BlockSpec((None, d, de), lambda i, be, tok, nu: (be[i], 0, 0)),
            pl.BlockSpec((None, de, d), lambda i, be, tok, nu: (be[i], 0, 0)),
        ],
        out_specs=pl.BlockSpec((MOE_BLOCK, d), lambda i, be, tok, nu: (i, 0)),
        scratch_shapes=[pltpu.VMEM((2, MOE_BLOCK, half), jnp.uint32), pltpu.SemaphoreType.DMA((2,))],
    )
    return pl.pallas_call(
        _moe_kernel,
        out_shape=jax.ShapeDtypeStruct((n_blocks * MOE_BLOCK, d), F32),
        grid_spec=grid_spec,
        compiler_params=_params(("arbitrary",)),
        name="moe_experts",
    )(block_e, buf_tok, n_used, hp2d, wg, wu, wd)


def _combine_kernel(pos_ref, ys_hbm, route_ref, x_ref, g_ref, lng_ref, lnb_ref, *rest, tm, with_h):
    if with_h:
        sc_ref, sh_ref, xo_ref, ho_ref, yb_ref, sem = rest
    else:
        xo_ref, yb_ref, sem = rest
    nt = pl.num_programs(1)
    step = pl.program_id(0) * nt + pl.program_id(1)
    total = pl.num_programs(0) * nt

    def row_copy(p, slot, k, r):
        return pltpu.make_async_copy(ys_hbm.at[pl.ds(p, 1)], yb_ref.at[slot, k, pl.ds(r, 1)], sem.at[slot])

    def issue(st, slot):
        base = st * tm

        def body(r, carry):
            for k in range(TOP_K):
                row_copy(pos_ref[TOP_K * (base + r) + k], slot, k, r).start()
            return carry

        lax.fori_loop(0, tm, body, 0, unroll=4)

    def wait_all(slot):
        def body(r, carry):
            for k in range(TOP_K):
                row_copy(0, slot, k, r).wait()
            return carry

        lax.fori_loop(0, tm, body, 0, unroll=4)

    @pl.when(step == 0)
    def _():
        issue(0, 0)

    @pl.when(step + 1 < total)
    def _():
        issue(step + 1, (step + 1) % 2)

    slot = step % 2
    wait_all(slot)
    route = route_ref[...]
    f = route[:, 2:3] * yb_ref[slot, 0] + route[:, 3:4] * yb_ref[slot, 1]
    v = ALPHA * x_ref[...] + g_ref[...] * f
    xn = _layer_norm_rows(v, lng_ref[...], lnb_ref[...])
    xo_ref[...] = xn
    if with_h:
        ho_ref[...] = (xn * (1.0 + sc_ref[...]) + sh_ref[...]).astype(ho_ref.dtype)


def _moe_combine(ys, pos, route, x, gate, lng, lnb, next_mod=None, tm=256):
    b, s, d = x.shape
    tm = min(tm, s)
    with_h = next_mod is not None
    row = lambda w: pl.BlockSpec((None, tm, w), lambda bi, i, pos: (bi, i, 0))
    per_b = lambda: pl.BlockSpec((None, 1, d), lambda bi, i, pos: (bi, 0, 0))
    const = lambda: pl.BlockSpec((1, d), lambda bi, i, pos: (0, 0))
    in_specs = [pl.BlockSpec(memory_space=pl.ANY), row(LANES), row(d), per_b(), const(), const()]
    out_specs, out_shape = [row(d)], [jax.ShapeDtypeStruct((b, s, d), F32)]
    args = [pos, ys, route, x, gate, lng, lnb]
    if with_h:
        in_specs += [per_b(), per_b()]
        out_specs.append(row(d))
        out_shape.append(jax.ShapeDtypeStruct((b, s, d), BF16))
        args += list(next_mod)
    grid_spec = pltpu.PrefetchScalarGridSpec(
        num_scalar_prefetch=1,
        grid=(b, s // tm),
        in_specs=in_specs,
        out_specs=tuple(out_specs),
        scratch_shapes=[pltpu.VMEM((2, TOP_K, tm, d), F32), pltpu.SemaphoreType.DMA((2,))],
    )
    kern = functools.partial(_combine_kernel, tm=tm, with_h=with_h)
    return pl.pallas_call(
        kern,
        out_shape=tuple(out_shape),
        grid_spec=grid_spec,
        compiler_params=_params(("arbitrary", "arbitrary")),
        name="moe_combine",
    )(*args)


def _conv_kernel(prev_ref, cur_ref, next_ref, w_ref, b_ref, g_ref, beta_ref, o_ref, buf_ref, acc_ref,
                 *, ts, tiles_per_seq, rs, cw):
    i = pl.program_id(0)
    first = (i % tiles_per_seq) == 0
    last = (i % tiles_per_seq) == tiles_per_seq - 1
    buf_ref[0:HALO, :] = jnp.where(first, 0.0, prev_ref[...])
    buf_ref[HALO:HALO + ts, :] = cur_ref[...]
    buf_ref[HALO + ts:, :] = jnp.where(last, 0.0, next_ref[...])
    d = cur_ref.shape[1]
    off = HALO - CONV_PAD

    def chunk(ci, carry):
        c0 = pl.multiple_of(ci * cw, cw)
        for r0 in range(0, ts, rs):
            acc = jnp.zeros((rs, cw), F32)
            for k in range(CONV_WIDTH):
                acc = acc + w_ref[k:k + 1, pl.ds(c0, cw)] * buf_ref[r0 + k + off:r0 + k + off + rs, pl.ds(c0, cw)]
            acc_ref[r0:r0 + rs, pl.ds(c0, cw)] = acc
        return carry

    lax.fori_loop(0, d // cw, chunk, 0)
    u = acc_ref[...] + b_ref[...]
    z = _layer_norm_rows(u, g_ref[...], beta_ref[...])
    o_ref[...] = (z * _sigmoid(z)).astype(o_ref.dtype)


def _conv_ln_swish(u2d, seq_len, w_dw, b_dw, ln_g, ln_b, ts=256, rs=64, cw=128):
    n, d = u2d.shape
    ts = min(ts, seq_len)
    rs = min(rs, ts)
    tiles_per_seq = seq_len // ts
    hb = ts // HALO
    n_halo = n // HALO
    w_pad = jnp.zeros((CONV_WIDTH + 1, d), F32).at[:CONV_WIDTH].set(w_dw)
    kern = functools.partial(_conv_kernel, ts=ts, tiles_per_seq=tiles_per_seq, rs=rs, cw=cw)
    const = lambda r: pl.BlockSpec((r, d), lambda i: (0, 0))
    return pl.pallas_call(
        kern,
        out_shape=jax.ShapeDtypeStruct((n, d), BF16),
        grid=(n // ts,),
        in_specs=[
            pl.BlockSpec((HALO, d), lambda i: (jnp.maximum(i * hb - 1, 0), 0)),
            pl.BlockSpec((ts, d), lambda i: (i, 0)),
            pl.BlockSpec((HALO, d), lambda i: (jnp.minimum((i + 1) * hb, n_halo - 1), 0)),
            const(CONV_WIDTH + 1), const(1), const(1), const(1),
        ],
        out_specs=pl.BlockSpec((ts, d), lambda i: (i, 0)),
        scratch_shapes=[pltpu.VMEM((ts + 2 * HALO, d), F32), pltpu.VMEM((ts, d), F32)],
        compiler_params=_params(("parallel",)),
        name="conv_ln_swish",
    )(u2d, u2d, u2d, w_pad, b_dw.reshape(1, d), ln_g.reshape(1, d), ln_b.reshape(1, d))


def _rope_tables(seq_len):
    inv = ROPE_THETA ** (-jnp.arange(N_FREQ, dtype=F32) / N_FREQ)
    n_rows = seq_len // GRID_W
    row = jnp.repeat(jnp.arange(n_rows, dtype=F32), GRID_W)
    col = jnp.tile(jnp.arange(GRID_W, dtype=F32), n_rows)
    ang_r, ang_c = row[:, None] * inv, col[:, None] * inv
    cr, sr, cc, sc = jnp.cos(ang_r), jnp.sin(ang_r), jnp.cos(ang_c), jnp.sin(ang_c)
    z = jnp.zeros_like(sr)
    cos = jnp.concatenate([cr, cr, cc, cc], axis=-1)
    sina = jnp.concatenate([-sr, z, -sc, z], axis=-1)
    sinb = jnp.concatenate([z, sr, z, sc], axis=-1)
    return cos, sina, sinb


def _routing_plan(route2d):
    n = route2d.shape[0]
    a = n * TOP_K
    eid_f = route2d[:, :TOP_K].astype(jnp.int32).reshape(a)
    tok_f = jnp.repeat(jnp.arange(n, dtype=jnp.int32), TOP_K)
    order = jnp.argsort(eid_f)
    se, stok = eid_f[order], tok_f[order]
    counts = jnp.bincount(eid_f, length=N_EXPERTS).astype(jnp.int32)
    pcounts = (counts + MOE_BLOCK - 1) // MOE_BLOCK * MOE_BLOCK
    starts = jnp.cumsum(counts) - counts
    pends = jnp.cumsum(pcounts)
    pstarts = pends - pcounts
    dest = pstarts[se] + jnp.arange(a, dtype=jnp.int32) - starts[se]
    n_blocks = -(-a // MOE_BLOCK) + N_EXPERTS
    buf_tok = jnp.zeros((n_blocks * MOE_BLOCK,), jnp.int32).at[dest].set(stok)
    pos = jnp.zeros((a,), jnp.int32).at[order].set(dest)
    block_e = jnp.minimum(jnp.searchsorted(pends, jnp.arange(n_blocks, dtype=jnp.int32) * MOE_BLOCK, side='right'),
                          N_EXPERTS - 1).astype(jnp.int32)
    n_used = (pends[-1:] // MOE_BLOCK).astype(jnp.int32)
    return block_e, buf_tok, n_used, pos


def _moe_layer(x, y, gate1, lng1, lnb1, sc2, sh2, gate2, lng2, lnb2, next_mod,
               w_rg, b_rg, w_re, b_re, w_gate, w_up, w_down):
    b, s, d = x.shape
    n = b * s
    pad = LANES - N_GROUPS - N_EXPERTS
    wr = jnp.concatenate([w_rg, w_re, jnp.zeros((d, pad), F32)], axis=1)
    br = jnp.concatenate([b_rg, b_re, jnp.zeros((pad,), F32)]).reshape(1, LANES)
    x1, hp, route = _ln_router(x, y.reshape(b, s, d), gate1, lng1, lnb1, sc2, sh2, wr, br)
    block_e, buf_tok, n_used, pos = _routing_plan(route.reshape(n, LANES))
    ys = _moe_experts(hp.reshape(n, d // 2), block_e, buf_tok, n_used,
                      w_gate.astype(BF16), w_up.astype(BF16), w_down.astype(BF16))
    return _moe_combine(ys, pos, route, x1, gate2, lng2, lnb2, next_mod)


def kernel(x, c, ctx, c_ctx, w_ada, b_ada, ln_g, ln_b, w_qkv, q_gain, k_gain, w_o, w_pw1, b_pw1, w_dw, b_dw,
           conv_ln_g, conv_ln_b, w_pw2, w_rg, b_rg, w_re, b_re, w_gate, w_up, w_down):
    b, s, d = x.shape
    cl = ctx.shape[1]
    n = b * s
    n_heads = d // HEAD_DIM
    n_kv = n_heads // GQA_GROUP
    q_dim, kv_dim = n_heads * HEAD_DIM, n_kv * HEAD_DIM
    assert w_ada.shape[0] == DEPTH and b + 1 <= ADA_ROWS

    cond = jnp.zeros((ADA_ROWS, d), F32).at[:b].set(c).at[b].set(c_ctx)
    mods = _ada_modulation(cond, w_ada, b_ada)

    def mod_chunks(layer, rows):
        m = mods[layer, rows].reshape(-1, 6, d)
        return [m[:, k][:, None, :] for k in range(6)]

    row = lambda v: v.reshape(1, d)

    sh1, sc1, g1, sh2, sc2, g2 = mod_chunks(0, slice(0, b))
    csh1, csc1 = [jnp.broadcast_to(m, (b, 1, d)) for m in mod_chunks(0, slice(b, b + 1))[:2]]
    nsh1, nsc1, ng1, nsh2, nsc2, ng2 = mod_chunks(1, slice(0, b))

    h = _modulate(x, sc1, sh1)
    hc = _modulate(ctx, csc1, csh1)
    wqkv = w_qkv[0].astype(BF16)
    ones_kv = jnp.ones((kv_dim,), F32)
    gain = jnp.concatenate([jnp.tile(q_gain[0], n_heads), jnp.tile(k_gain[0], n_kv), ones_kv]).reshape(1, -1)
    scale = jnp.concatenate([jnp.full((q_dim,), ATTN_SCALE, F32), ones_kv, ones_kv]).reshape(1, -1)
    cos, sina, sinb = _rope_tables(s)
    qkv = _qkv_project(h.reshape(n, d), wqkv, gain, scale, cos, sina, sinb, q_dim=q_dim, kv_dim=kv_dim,
                       col_off_cols=0, n_cols=q_dim + 2 * kv_dim, tm=1024)
    one_c, zero_c = jnp.ones((cl, HEAD_DIM), F32), jnp.zeros((cl, HEAD_DIM), F32)
    kvc = _qkv_project(hc.reshape(b * cl, d), wqkv, gain, scale, one_c, zero_c, zero_c, q_dim=q_dim, kv_dim=kv_dim,
                       col_off_cols=q_dim, n_cols=2 * kv_dim, tm=cl)
    o = _attention(qkv.reshape(b, s, -1), kvc.reshape(b, cl, -1), n_heads=n_heads, n_kv=n_kv)
    y = _matmul(o.reshape(n, d), w_o[0].astype(BF16))
    x, hn = _moe_layer(x, y, g1, row(ln_g[0, 0]), row(ln_b[0, 0]), sc2, sh2, g2, row(ln_g[0, 1]), row(ln_b[0, 1]),
                       (nsc1, nsh1), w_rg[0], b_rg[0], w_re[0], b_re[0], w_gate[0], w_up[0], w_down[0])

    u = _glu_matmul(hn.reshape(n, d), w_pw1[0].astype(BF16), b_pw1[0])
    cv = _conv_ln_swish(u, s, w_dw[0], b_dw[0], conv_ln_g[0], conv_ln_b[0])
    y = _matmul(cv, w_pw2[0].astype(BF16))
    (x,) = _moe_layer(x, y, ng1, row(ln_g[1, 0]), row(ln_b[1, 0]), nsc2, nsh2, ng2, row(ln_g[1, 1]), row(ln_b[1, 1]),
                      None, w_rg[1], b_rg[1], w_re[1], b_re[1], w_gate[1], w_up[1], w_down[1])
    return x
```

```python
import functools

import jax
import jax.numpy as jnp
from jax import lax
from jax.experimental import pallas as pl
from jax.experimental.pallas import tpu as pltpu

F32 = jnp.float32
BF16 = jnp.bfloat16

HEAD_DIM = 128
GQA_GROUP = 4
GRID_W = 64
ROPE_THETA = 10000.0
N_FREQ = HEAD_DIM // 4
ATTN_SCALE = HEAD_DIM ** -0.5
LOG2_E = 1.4426950408889634
CONV_WIDTH = 31
CONV_PAD = CONV_WIDTH // 2
N_GROUPS = 4
EXPERTS_PER_GROUP = 8
N_EXPERTS = N_GROUPS * EXPERTS_PER_GROUP
TOP_K = 2
MOE_BLOCK = 256
LN_EPS = 1e-5
QK_EPS = 1e-6
DEPTH = 2
ALPHA = (2 * DEPTH) ** 0.25

LANES = 128
HALO = 16
ADA_ROWS = 8
VMEM_LIMIT = 56 * 1024 * 1024


def _params(sem, vmem=VMEM_LIMIT, flags=None):
    return pltpu.CompilerParams(dimension_semantics=sem, vmem_limit_bytes=vmem, flags=flags)


def _sigmoid(x):
    return 1.0 / (1.0 + jnp.exp(-x))


def _dot(a, b):
    return jnp.dot(a, b, preferred_element_type=F32)


def _ada_kernel(c_ref, w_ref, b_ref, o_ref):
    c = c_ref[...]
    s = (c * _sigmoid(c)).astype(BF16)
    o_ref[...] = _dot(s, w_ref[...].astype(BF16)) + b_ref[...]


def _ada_modulation(cond, w_ada, b_ada, tn=512):
    depth, d, n6 = w_ada.shape
    return pl.pallas_call(
        _ada_kernel,
        out_shape=jax.ShapeDtypeStruct((depth, ADA_ROWS, n6), F32),
        grid=(depth, n6 // tn),
        in_specs=[
            pl.BlockSpec((ADA_ROWS, d), lambda l, j: (0, 0)),
            pl.BlockSpec((None, d, tn), lambda l, j: (l, 0, j)),
            pl.BlockSpec((None, 1, tn), lambda l, j: (l, 0, j)),
        ],
        out_specs=pl.BlockSpec((None, ADA_ROWS, tn), lambda l, j: (l, 0, j)),
        compiler_params=_params(("parallel", "parallel")),
        name="ada_modulation",
    )(cond, w_ada, b_ada.reshape(depth, 1, n6))


def _modulate_kernel(x_ref, sc_ref, sh_ref, o_ref):
    o_ref[...] = (x_ref[...] * (1.0 + sc_ref[...]) + sh_ref[...]).astype(o_ref.dtype)


def _modulate(x, sc, sh, tm=256):
    b, s, d = x.shape
    tm = min(tm, s)
    return pl.pallas_call(
        _modulate_kernel,
        out_shape=jax.ShapeDtypeStruct((b, s, d), BF16),
        grid=(b, s // tm),
        in_specs=[
            pl.BlockSpec((None, tm, d), lambda bi, i: (bi, i, 0)),
            pl.BlockSpec((None, 1, d), lambda bi, i: (bi, 0, 0)),
            pl.BlockSpec((None, 1, d), lambda bi, i: (bi, 0, 0)),
        ],
        out_specs=pl.BlockSpec((None, tm, d), lambda bi, i: (bi, i, 0)),
        compiler_params=_params(("parallel", "parallel")),
        name="modulate",
    )(x, sc, sh)


def _qkv_kernel(x_ref, w_ref, gain_ref, scale_ref, cos_ref, sina_ref, sinb_ref, o_ref, *, first_v_tile, col_off):
    j = pl.program_id(1) + col_off
    y = _dot(x_ref[...], w_ref[...])
    tn = y.shape[1]

    @pl.when(j >= first_v_tile)
    def _():
        o_ref[...] = y.astype(o_ref.dtype)

    @pl.when(j < first_v_tile)
    def _():
        cos = cos_ref[...]
        sina = sina_ref[...]
        sinb = sinb_ref[...]
        for h in range(tn // HEAD_DIM):
            sl = slice(h * HEAD_DIM, (h + 1) * HEAD_DIM)
            yh = y[:, sl]
            ms = jnp.mean(yh * yh, axis=-1, keepdims=True)
            yn = yh * lax.rsqrt(ms + QK_EPS) * gain_ref[:, sl]
            rot = (yn * cos + pltpu.roll(yn, HEAD_DIM - N_FREQ, 1) * sina
                   + pltpu.roll(yn, N_FREQ, 1) * sinb)
            o_ref[:, sl] = (rot * scale_ref[:, sl]).astype(o_ref.dtype)


def _qkv_project(x2d, w, gain, scale, cos, sina, sinb, *, q_dim, kv_dim, col_off_cols, n_cols, tm, tn=512):
    m, d = x2d.shape
    tm = min(tm, m, cos.shape[0])
    tn = min(tn, kv_dim)
    pos_tiles = cos.shape[0] // tm
    col_off = col_off_cols // tn
    kern = functools.partial(_qkv_kernel, first_v_tile=(q_dim + kv_dim) // tn, col_off=col_off)
    return pl.pallas_call(
        kern,
        out_shape=jax.ShapeDtypeStruct((m, n_cols), BF16),
        grid=(m // tm, n_cols // tn),
        in_specs=[
            pl.BlockSpec((tm, d), lambda i, j: (i, 0)),
            pl.BlockSpec((d, tn), lambda i, j: (0, j + col_off)),
            pl.BlockSpec((1, tn), lambda i, j: (0, j + col_off)),
            pl.BlockSpec((1, tn), lambda i, j: (0, j + col_off)),
            pl.BlockSpec((tm, HEAD_DIM), lambda i, j: (i % pos_tiles, 0)),
            pl.BlockSpec((tm, HEAD_DIM), lambda i, j: (i % pos_tiles, 0)),
            pl.BlockSpec((tm, HEAD_DIM), lambda i, j: (i % pos_tiles, 0)),
        ],
        out_specs=pl.BlockSpec((tm, tn), lambda i, j: (i, j)),
        compiler_params=_params(("parallel", "parallel")),
        name="qkv_project",
    )(x2d, w, gain, scale, cos, sina, sinb)


def _attn_kernel(q_ref, k_ref, vt_ref, o_ref, qt_ref, s_bufs, p_bufs, al_bufs, m_ref, l_ref, acc_ref,
                 *, tq, tk, n_chunks, rg):
    for g in range(GQA_GROUP):
        qt_ref[:, g * tq:(g + 1) * tq] = q_ref[:, g * HEAD_DIM:(g + 1) * HEAD_DIM].T
    m_ref[...] = jnp.full(m_ref.shape, -jnp.inf, F32)
    l_ref[...] = jnp.zeros(l_ref.shape, F32)
    acc_ref[...] = jnp.zeros(acc_ref.shape, F32)

    def start(t):
        return pl.multiple_of(t * tk, tk)

    def scores(t, par):
        s_bufs[par] = _dot(k_ref[pl.ds(start(t), tk), :], qt_ref[...])

    def softmax(par):
        rows = s_bufs.shape[2]
        for c in range(rows // LANES):
            cs = slice(c * LANES, (c + 1) * LANES)
            mx = s_bufs[par, 0:rg, cs]
            for r in range(rg, tk, rg):
                mx = jnp.maximum(mx, s_bufs[par, r:r + rg, cs])
            m_prev = m_ref[:, cs]
            m_new = jnp.maximum(m_prev, jnp.max(mx, axis=0, keepdims=True))
            alpha = jnp.exp2(m_prev - m_new)
            psum = jnp.zeros((rg, LANES), F32)
            for r in range(0, tk, rg):
                p = jnp.exp2(s_bufs[par, r:r + rg, cs] - m_new)
                psum = psum + p
                p_bufs[par, r:r + rg, cs] = p.astype(BF16)
            l_ref[:, cs] = alpha * l_ref[:, cs] + jnp.sum(psum, axis=0, keepdims=True)
            m_ref[:, cs] = m_new
            al_bufs[par, :, cs] = alpha

    def values(t, par):
        acc_ref[...] = al_bufs[par] * acc_ref[...] + _dot(vt_ref[:, pl.ds(start(t), tk)], p_bufs[par])

    def stage(t, par, do_scores=True, do_values=True):
        if do_scores:
            scores(t + 1, 1 - par)
        softmax(par)
        if do_values:
            values(t - 1, 1 - par)

    n = n_chunks
    scores(0, 0)
    stage(0, 0, do_scores=n > 1, do_values=False)
    n_mid = max(n - 2, 0)
    n_pairs = n_mid // 2

    def pair(i, carry):
        t = 2 * i + 1
        stage(t, 1)
        stage(t + 1, 0)
        return carry

    if n_pairs > 2:
        lax.fori_loop(0, n_pairs, pair, 0)
    else:
        for i in range(n_pairs):
            pair(i, 0)
    t = 2 * n_pairs + 1
    if n_mid % 2:
        stage(t, 1)
        t += 1
    if n > 1:
        stage(t, t % 2, do_scores=False)
    values(n - 1, (n - 1) % 2)

    out = (acc_ref[...] / l_ref[...]).T
    for g in range(GQA_GROUP):
        o_ref[:, g * HEAD_DIM:(g + 1) * HEAD_DIM] = out[g * tq:(g + 1) * tq, :].astype(o_ref.dtype)


def _attention(q_src, k_all, vt_all, *, n_heads, n_kv, tq=256, tk=768):
    b, s, _ = q_src.shape
    sk = k_all.shape[1]
    tq = min(tq, s)
    assert sk % tk == 0 and s % tq == 0
    gw = GQA_GROUP * HEAD_DIM
    rows = GQA_GROUP * tq
    kern = functools.partial(_attn_kernel, tq=tq, tk=tk, n_chunks=sk // tk, rg=LANES)
    return pl.pallas_call(
        kern,
        out_shape=jax.ShapeDtypeStruct((b, s, n_heads * HEAD_DIM), BF16),
        grid=(b, n_kv, s // tq),
        in_specs=[
            pl.BlockSpec((None, tq, gw), lambda bi, h, i: (bi, i, h)),
            pl.BlockSpec((None, sk, HEAD_DIM), lambda bi, h, i: (bi, 0, h)),
            pl.BlockSpec((None, HEAD_DIM, sk), lambda bi, h, i: (bi, h, 0)),
        ],
        out_specs=pl.BlockSpec((None, tq, gw), lambda bi, h, i: (bi, i, h)),
        scratch_shapes=[
            pltpu.VMEM((HEAD_DIM, rows), BF16),
            pltpu.VMEM((2, tk, rows), F32),
            pltpu.VMEM((2, tk, rows), BF16),
            pltpu.VMEM((2, 1, rows), F32),
            pltpu.VMEM((1, rows), F32),
            pltpu.VMEM((1, rows), F32),
            pltpu.VMEM((HEAD_DIM, rows), F32),
        ],
        compiler_params=_params(("parallel", "parallel", "arbitrary")),
        name="attention",
    )(q_src, k_all, vt_all)


def _matmul_kernel(x_ref, w_ref, o_ref):
    o_ref[...] = _dot(x_ref[...], w_ref[...]).astype(o_ref.dtype)


def _matmul(x2d, w, tm=512, tn=1024):
    m, k = x2d.shape
    n = w.shape[1]
    tm, tn = min(tm, m), min(tn, n)
    return pl.pallas_call(
        _matmul_kernel,
        out_shape=jax.ShapeDtypeStruct((m, n), F32),
        grid=(m // tm, n // tn),
        in_specs=[pl.BlockSpec((tm, k), lambda i, j: (i, 0)), pl.BlockSpec((k, tn), lambda i, j: (0, j))],
        out_specs=pl.BlockSpec((tm, tn), lambda i, j: (i, j)),
        compiler_params=_params(("parallel", "parallel")),
        name="matmul",
    )(x2d, w)


def _glu_kernel(x_ref, wa_ref, wg_ref, ba_ref, bg_ref, o_ref):
    x = x_ref[...]
    a = _dot(x, wa_ref[...]) + ba_ref[...]
    g = _dot(x, wg_ref[...]) + bg_ref[...]
    o_ref[...] = a * _sigmoid(g)


def _glu_matmul(x2d, w, bias, tm=512, tn=512):
    m, k = x2d.shape
    n = w.shape[1] // 2
    tm, tn = min(tm, m), min(tn, n)
    nt = n // tn
    bias2 = bias.reshape(1, 2 * n)
    return pl.pallas_call(
        _glu_kernel,
        out_shape=jax.ShapeDtypeStruct((m, n), F32),
        grid=(m // tm, nt),
        in_specs=[
            pl.BlockSpec((tm, k), lambda i, j: (i, 0)),
            pl.BlockSpec((k, tn), lambda i, j: (0, j)),
            pl.BlockSpec((k, tn), lambda i, j: (0, j + nt)),
            pl.BlockSpec((1, tn), lambda i, j: (0, j)),
            pl.BlockSpec((1, tn), lambda i, j: (0, j + nt)),
        ],
        out_specs=pl.BlockSpec((tm, tn), lambda i, j: (i, j)),
        compiler_params=_params(("parallel", "parallel")),
        name="glu_matmul",
    )(x2d, w, w, bias2, bias2)


def _layer_norm_rows(v, g, b):
    mu = jnp.mean(v, axis=-1, keepdims=True)
    cen = v - mu
    var = jnp.mean(cen * cen, axis=-1, keepdims=True)
    return cen * lax.rsqrt(var + LN_EPS) * g + b


def _pack_bf16_pairs(h):
    half = h.shape[1] // 2
    hb = h.astype(BF16).astype(F32)
    lo = pltpu.bitcast(hb[:, :half], jnp.uint32)
    hi = pltpu.bitcast(hb[:, half:], jnp.uint32)
    return (lo >> 16) | (hi & jnp.uint32(0xFFFF0000))


def _unpack_bf16_pairs(w):
    lo = pltpu.bitcast(w << 16, F32).astype(BF16)
    hi = pltpu.bitcast(w & jnp.uint32(0xFFFF0000), F32).astype(BF16)
    return lo, hi


def _route(logits):
    lane = lax.broadcasted_iota(jnp.int32, logits.shape, 1).astype(F32)
    neg = -jnp.inf
    big = float(LANES)
    is_g = lane < N_GROUPS
    gl = jnp.where(is_g, logits, neg)
    gmax = jnp.max(gl, axis=-1, keepdims=True)
    gidx = jnp.min(jnp.where(gl == gmax, lane, big), axis=-1, keepdims=True)
    gsum = jnp.sum(jnp.where(is_g, jnp.exp(gl - gmax), 0.0), axis=-1, keepdims=True)
    gw = 1.0 / gsum
    lo = N_GROUPS + EXPERTS_PER_GROUP * gidx
    el = jnp.where(lane >= lo, jnp.where(lane < lo + EXPERTS_PER_GROUP, logits, neg), neg)
    t1 = jnp.max(el, axis=-1, keepdims=True)
    i1 = jnp.min(jnp.where(el == t1, lane, big), axis=-1, keepdims=True)
    el2 = jnp.where(lane == i1, neg, el)
    t2 = jnp.max(el2, axis=-1, keepdims=True)
    i2 = jnp.min(jnp.where(el2 == t2, lane, big), axis=-1, keepdims=True)
    e = jnp.exp(t2 - t1)
    w1 = gw / (1.0 + e)
    w2 = gw * e / (1.0 + e)
    out = jnp.where(lane == 0.0, i1 - N_GROUPS, 0.0)
    out = jnp.where(lane == 1.0, i2 - N_GROUPS, out)
    out = jnp.where(lane == 2.0, w1, out)
    out = jnp.where(lane == 3.0, w2, out)
    return out


def _ln_router_kernel(x_ref, y_ref, g_ref, lng_ref, lnb_ref, sc_ref, sh_ref, wr_ref, br_ref,
                      xo_ref, hp_ref, route_ref):
    v = ALPHA * x_ref[...] + g_ref[...] * y_ref[...]
    xn = _layer_norm_rows(v, lng_ref[...], lnb_ref[...])
    xo_ref[...] = xn
    h = xn * (1.0 + sc_ref[...]) + sh_ref[...]
    hp_ref[...] = _pack_bf16_pairs(h)
    h_hi = h.astype(BF16)
    h_lo = (h - h_hi.astype(F32)).astype(BF16)
    wr = wr_ref[...]
    w_hi = wr.astype(BF16)
    w_lo = (wr - w_hi.astype(F32)).astype(BF16)
    logits = _dot(h_hi, w_hi) + _dot(h_lo, w_hi) + _dot(h_hi, w_lo) + br_ref[...]
    route_ref[...] = _route(logits)


def _ln_router(x, y, gate, lng, lnb, sc, sh, wr, br, tm=256):
    b, s, d = x.shape
    tm = min(tm, s)
    row = lambda: pl.BlockSpec((None, tm, d), lambda bi, i: (bi, i, 0))
    per_b = lambda: pl.BlockSpec((None, 1, d), lambda bi, i: (bi, 0, 0))
    const = lambda shape: pl.BlockSpec(shape, lambda bi, i: (0,) * len(shape))
    return pl.pallas_call(
        _ln_router_kernel,
        out_shape=(
            jax.ShapeDtypeStruct((b, s, d), F32),
            jax.ShapeDtypeStruct((b, s, d // 2), jnp.uint32),
            jax.ShapeDtypeStruct((b, s, LANES), F32),
        ),
        grid=(b, s // tm),
        in_specs=[row(), row(), per_b(), const((1, d)), const((1, d)), per_b(), per_b(),
                  const((d, LANES)), const((1, LANES))],
        out_specs=(
            row(),
            pl.BlockSpec((None, tm, d // 2), lambda bi, i: (bi, i, 0)),
            pl.BlockSpec((None, tm, LANES), lambda bi, i: (bi, i, 0)),
        ),
        compiler_params=_params(("parallel", "parallel")),
        name="ln_router",
    )(x, y, gate, lng, lnb, sc, sh, wr, br)


def _moe_kernel(be_ref, tok_ref, nu_ref, h_hbm, wg_ref, wu_ref, wd_ref, o_ref, xb_ref, sem):
    i = pl.program_id(0)
    n_used = nu_ref[0]
    half = xb_ref.shape[2]

    def row_copy(tok, slot, r):
        return pltpu.make_async_copy(h_hbm.at[pl.ds(tok, 1)], xb_ref.at[slot, pl.ds(r, 1)], sem.at[slot])

    def issue(blk, slot):
        base = blk * MOE_BLOCK

        def body(r, carry):
            row_copy(tok_ref[base + r], slot, r).start()
            return carry

        lax.fori_loop(0, MOE_BLOCK, body, 0, unroll=8)

    def wait_all(slot):
        def body(r, carry):
            row_copy(0, slot, r).wait()
            return carry

        lax.fori_loop(0, MOE_BLOCK, body, 0, unroll=8)

    @pl.when(i == 0)
    def _():
        issue(0, 0)

    @pl.when(i + 1 < n_used)
    def _():
        issue(i + 1, (i + 1) % 2)

    @pl.when(i < n_used)
    def _():
        slot = i % 2
        wait_all(slot)
        lo, hi = _unpack_bf16_pairs(xb_ref[slot])
        gate = _dot(lo, wg_ref[:half, :]) + _dot(hi, wg_ref[half:, :])
        up = _dot(lo, wu_ref[:half, :]) + _dot(hi, wu_ref[half:, :])
        act = (gate * _sigmoid(gate) * up).astype(BF16)
        o_ref[...] = _dot(act, wd_ref[...])

    @pl.when(i >= n_used)
    def _():
        o_ref[...] = jnp.zeros(o_ref.shape, o_ref.dtype)


def _moe_experts(hp2d, block_e, buf_tok, n_used, wg, wu, wd):
    n_blocks = block_e.shape[0]
    half = hp2d.shape[1]
    d = 2 * half
    de = wg.shape[2]
    grid_spec = pltpu.PrefetchScalarGridSpec(
        num_scalar_prefetch=3,
        grid=(n_blocks,),
        in_specs=[
            pl.BlockSpec(memory_space=pl.ANY),
            pl.BlockSpec((None, d, de), lambda i, be, tok, nu: (be[i], 0, 0)),
            pl.BlockSpec((None, d, de), lambda i, be, tok, nu: (be[i], 0, 0)),
            pl.BlockSpec((None, de, d), lambda i, be, tok, nu: (be[i], 0, 0)),
        ],
        out_specs=pl.BlockSpec((MOE_BLOCK, d), lambda i, be, tok, nu: (i, 0)),
        scratch_shapes=[pltpu.VMEM((2, MOE_BLOCK, half), jnp.uint32), pltpu.SemaphoreType.DMA((2,))],
    )
    return pl.pallas_call(
        _moe_kernel,
        out_shape=jax.ShapeDtypeStruct((n_blocks * MOE_BLOCK, d), F32),
        grid_spec=grid_spec,
        compiler_params=_params(("arbitrary",)),
        name="moe_experts",
    )(block_e, buf_tok, n_used, hp2d, wg, wu, wd)


def _combine_kernel(pos_ref, ys_hbm, route_ref, x_ref, g_ref, lng_ref, lnb_ref, *rest, tm, with_h):
    if with_h:
        sc_ref, sh_ref, xo_ref, ho_ref, yb_ref, sem = rest
    else:
        xo_ref, yb_ref, sem = rest
    nt = pl.num_programs(1)
    step = pl.program_id(0) * nt + pl.program_id(1)
    total = pl.num_programs(0) * nt

    def row_copy(p, slot, k, r):
        return pltpu.make_async_copy(ys_hbm.at[pl.ds(p, 1)], yb_ref.at[slot, k, pl.ds(r, 1)], sem.at[slot])

    def issue(st, slot):
        base = st * tm

        def body(r, carry):
            for k in range(TOP_K):
                row_copy(pos_ref[TOP_K * (base + r) + k], slot, k, r).start()
            return carry

        lax.fori_loop(0, tm, body, 0, unroll=4)

    def wait_all(slot):
        def body(r, carry):
            for k in range(TOP_K):
                row_copy(0, slot, k, r).wait()
            return carry

        lax.fori_loop(0, tm, body, 0, unroll=4)

    @pl.when(step == 0)
    def _():
        issue(0, 0)

    @pl.when(step + 1 < total)
    def _():
        issue(step + 1, (step + 1) % 2)

    slot = step % 2
    wait_all(slot)
    route = route_ref[...]
    f = route[:, 2:3] * yb_ref[slot, 0] + route[:, 3:4] * yb_ref[slot, 1]
    v = ALPHA * x_ref[...] + g_ref[...] * f
    xn = _layer_norm_rows(v, lng_ref[...], lnb_ref[...])
    xo_ref[...] = xn
    if with_h:
        ho_ref[...] = (xn * (1.0 + sc_ref[...]) + sh_ref[...]).astype(ho_ref.dtype)


def _moe_combine(ys, pos, route, x, gate, lng, lnb, next_mod=None, tm=256):
    b, s, d = x.shape
    tm = min(tm, s)
    with_h = next_mod is not None
    row = lambda w: pl.BlockSpec((None, tm, w), lambda bi, i, pos: (bi, i, 0))
    per_b = lambda: pl.BlockSpec((None, 1, d), lambda bi, i, pos: (bi, 0, 0))
    const = lambda: pl.BlockSpec((1, d), lambda bi, i, pos: (0, 0))
    in_specs = [pl.BlockSpec(memory_space=pl.ANY), row(LANES), row(d), per_b(), const(), const()]
    out_specs, out_shape = [row(d)], [jax.ShapeDtypeStruct((b, s, d), F32)]
    args = [pos, ys, route, x, gate, lng, lnb]
    if with_h:
        in_specs += [per_b(), per_b()]
        out_specs.append(row(d))
        out_shape.append(jax.ShapeDtypeStruct((b, s, d), BF16))
        args += list(next_mod)
    grid_spec = pltpu.PrefetchScalarGridSpec(
        num_scalar_prefetch=1,
        grid=(b, s // tm),
        in_specs=in_specs,
        out_specs=tuple(out_specs),
        scratch_shapes=[pltpu.VMEM((2, TOP_K, tm, d), F32), pltpu.SemaphoreType.DMA((2,))],
    )
    kern = functools.partial(_combine_kernel, tm=tm, with_h=with_h)
    return pl.pallas_call(
        kern,
        out_shape=tuple(out_shape),
        grid_spec=grid_spec,
        compiler_params=_params(("arbitrary", "arbitrary")),
        name="moe_combine",
    )(*args)


def _conv_kernel(prev_ref, cur_ref, next_ref, w_ref, b_ref, g_ref, beta_ref, o_ref, buf_ref, acc_ref,
                 *, ts, tiles_per_seq, rs, cw):
    i = pl.program_id(0)
    first = (i % tiles_per_seq) == 0
    last = (i % tiles_per_seq) == tiles_per_seq - 1
    buf_ref[0:HALO, :] = jnp.where(first, 0.0, prev_ref[...])
    buf_ref[HALO:HALO + ts, :] = cur_ref[...]
    buf_ref[HALO + ts:, :] = jnp.where(last, 0.0, next_ref[...])
    d = cur_ref.shape[1]
    off = HALO - CONV_PAD

    def chunk(ci, carry):
        c0 = pl.multiple_of(ci * cw, cw)
        for r0 in range(0, ts, rs):
            acc = jnp.zeros((rs, cw), F32)
            for k in range(CONV_WIDTH):
                acc = acc + w_ref[k:k + 1, pl.ds(c0, cw)] * buf_ref[r0 + k + off:r0 + k + off + rs, pl.ds(c0, cw)]
            acc_ref[r0:r0 + rs, pl.ds(c0, cw)] = acc
        return carry

    lax.fori_loop(0, d // cw, chunk, 0)
    u = acc_ref[...] + b_ref[...]
    z = _layer_norm_rows(u, g_ref[...], beta_ref[...])
    o_ref[...] = (z * _sigmoid(z)).astype(o_ref.dtype)


def _conv_ln_swish(u2d, seq_len, w_dw, b_dw, ln_g, ln_b, ts=256, rs=64, cw=128):
    n, d = u2d.shape
    ts = min(ts, seq_len)
    rs = min(rs, ts)
    tiles_per_seq = seq_len // ts
    hb = ts // HALO
    n_halo = n // HALO
    w_pad = jnp.zeros((CONV_WIDTH + 1, d), F32).at[:CONV_WIDTH].set(w_dw)
    kern = functools.partial(_conv_kernel, ts=ts, tiles_per_seq=tiles_per_seq, rs=rs, cw=cw)
    const = lambda r: pl.BlockSpec((r, d), lambda i: (0, 0))
    return pl.pallas_call(
        kern,
        out_shape=jax.ShapeDtypeStruct((n, d), BF16),
        grid=(n // ts,),
        in_specs=[
            pl.BlockSpec((HALO, d), lambda i: (jnp.maximum(i * hb - 1, 0), 0)),
            pl.BlockSpec((ts, d), lambda i: (i, 0)),
            pl.BlockSpec((HALO, d), lambda i: (jnp.minimum((i + 1) * hb, n_halo - 1), 0)),
            const(CONV_WIDTH + 1), const(1), const(1), const(1),
        ],
        out_specs=pl.BlockSpec((ts, d), lambda i: (i, 0)),
        scratch_shapes=[pltpu.VMEM((ts + 2 * HALO, d), F32), pltpu.VMEM((ts, d), F32)],
        compiler_params=_params(("parallel",)),
        name="conv_ln_swish",
    )(u2d, u2d, u2d, w_pad, b_dw.reshape(1, d), ln_g.reshape(1, d), ln_b.reshape(1, d))


def _rope_tables(seq_len):
    inv = ROPE_THETA ** (-jnp.arange(N_FREQ, dtype=F32) / N_FREQ)
    n_rows = seq_len // GRID_W
    row = jnp.repeat(jnp.arange(n_rows, dtype=F32), GRID_W)
    col = jnp.tile(jnp.arange(GRID_W, dtype=F32), n_rows)
    ang_r, ang_c = row[:, None] * inv, col[:, None] * inv
    cr, sr, cc, sc = jnp.cos(ang_r), jnp.sin(ang_r), jnp.cos(ang_c), jnp.sin(ang_c)
    z = jnp.zeros_like(sr)
    cos = jnp.concatenate([cr, cr, cc, cc], axis=-1)
    sina = jnp.concatenate([-sr, z, -sc, z], axis=-1)
    sinb = jnp.concatenate([z, sr, z, sc], axis=-1)
    return cos, sina, sinb


def _routing_plan(route2d):
    n = route2d.shape[0]
    a = n * TOP_K
    eid_f = route2d[:, :TOP_K].astype(jnp.int32).reshape(a)
    tok_f = jnp.repeat(jnp.arange(n, dtype=jnp.int32), TOP_K)
    order = jnp.argsort(eid_f)
    se, stok = eid_f[order], tok_f[order]
    counts = jnp.bincount(eid_f, length=N_EXPERTS).astype(jnp.int32)
    pcounts = (counts + MOE_BLOCK - 1) // MOE_BLOCK * MOE_BLOCK
    starts = jnp.cumsum(counts) - counts
    pends = jnp.cumsum(pcounts)
    pstarts = pends - pcounts
    dest = pstarts[se] + jnp.arange(a, dtype=jnp.int32) - starts[se]
    n_blocks = -(-a // MOE_BLOCK) + N_EXPERTS
    buf_tok = jnp.zeros((n_blocks * MOE_BLOCK,), jnp.int32).at[dest].set(stok)
    pos = jnp.zeros((a,), jnp.int32).at[order].set(dest)
    block_e = jnp.minimum(jnp.searchsorted(pends, jnp.arange(n_blocks, dtype=jnp.int32) * MOE_BLOCK, side='right'),
                          N_EXPERTS - 1).astype(jnp.int32)
    n_used = (pends[-1:] // MOE_BLOCK).astype(jnp.int32)
    return block_e, buf_tok, n_used, pos


def _moe_layer(x, y, gate1, lng1, lnb1, sc2, sh2, gate2, lng2, lnb2, next_mod,
               w_rg, b_rg, w_re, b_re, w_gate, w_up, w_down):
    b, s, d = x.shape
    n = b * s
    pad = LANES - N_GROUPS - N_EXPERTS
    wr = jnp.concatenate([w_rg, w_re, jnp.zeros((d, pad), F32)], axis=1)
    br = jnp.concatenate([b_rg, b_re, jnp.zeros((pad,), F32)]).reshape(1, LANES)
    x1, hp, route = _ln_router(x, y.reshape(b, s, d), gate1, lng1, lnb1, sc2, sh2, wr, br)
    block_e, buf_tok, n_used, pos = _routing_plan(route.reshape(n, LANES))
    ys = _moe_experts(hp.reshape(n, d // 2), block_e, buf_tok, n_used,
                      w_gate.astype(BF16), w_up.astype(BF16), w_down.astype(BF16))
    return _moe_combine(ys, pos, route, x1, gate2, lng2, lnb2, next_mod)


def kernel(x, c, ctx, c_ctx, w_ada, b_ada, ln_g, ln_b, w_qkv, q_gain, k_gain, w_o, w_pw1, b_pw1, w_dw, b_dw,
           conv_ln_g, conv_ln_b, w_pw2, w_rg, b_rg, w_re, b_re, w_gate, w_up, w_down):
    b, s, d = x.shape
    cl = ctx.shape[1]
    n = b * s
    n_heads = d // HEAD_DIM
    n_kv = n_heads // GQA_GROUP
    q_dim, kv_dim = n_heads * HEAD_DIM, n_kv * HEAD_DIM
    assert w_ada.shape[0] == DEPTH and b + 1 <= ADA_ROWS

    cond = jnp.zeros((ADA_ROWS, d), F32).at[:b].set(c).at[b].set(c_ctx)
    mods = _ada_modulation(cond, w_ada, b_ada)

    def mod_chunks(layer, rows):
        m = mods[layer, rows].reshape(-1, 6, d)
        return [m[:, k][:, None, :] for k in range(6)]

    row = lambda v: v.reshape(1, d)

    sh1, sc1, g1, sh2, sc2, g2 = mod_chunks(0, slice(0, b))
    csh1, csc1 = [jnp.broadcast_to(m, (b, 1, d)) for m in mod_chunks(0, slice(b, b + 1))[:2]]
    nsh1, nsc1, ng1, nsh2, nsc2, ng2 = mod_chunks(1, slice(0, b))

    h = _modulate(x, sc1, sh1)
    hc = _modulate(ctx, csc1, csh1)
    wqkv = w_qkv[0].astype(BF16)
    ones_kv = jnp.ones((kv_dim,), F32)
    gain = jnp.concatenate([jnp.tile(q_gain[0], n_heads), jnp.tile(k_gain[0], n_kv), ones_kv]).reshape(1, -1)
    scale = jnp.concatenate([jnp.full((q_dim,), ATTN_SCALE * LOG2_E, F32), ones_kv, ones_kv]).reshape(1, -1)
    cos, sina, sinb = _rope_tables(s)
    qkv = _qkv_project(h.reshape(n, d), wqkv, gain, scale, cos, sina, sinb, q_dim=q_dim, kv_dim=kv_dim,
                       col_off_cols=0, n_cols=q_dim + 2 * kv_dim, tm=1024)
    one_c, zero_c = jnp.ones((cl, HEAD_DIM), F32), jnp.zeros((cl, HEAD_DIM), F32)
    kvc = _qkv_project(hc.reshape(b * cl, d), wqkv, gain, scale, one_c, zero_c, zero_c, q_dim=q_dim, kv_dim=kv_dim,
                       col_off_cols=q_dim, n_cols=2 * kv_dim, tm=cl)
    qkv, kvc = qkv.reshape(b, s, -1), kvc.reshape(b, cl, -1)
    k_all = jnp.concatenate([qkv[:, :, q_dim:q_dim + kv_dim], kvc[:, :, :kv_dim]], axis=1)
    v_all = jnp.concatenate([qkv[:, :, q_dim + kv_dim:], kvc[:, :, kv_dim:]], axis=1)
    o = _attention(qkv, k_all, jnp.swapaxes(v_all, 1, 2), n_heads=n_heads, n_kv=n_kv)
    y = _matmul(o.reshape(n, d), w_o[0].astype(BF16))
    x, hn = _moe_layer(x, y, g1, row(ln_g[0, 0]), row(ln_b[0, 0]), sc2, sh2, g2, row(ln_g[0, 1]), row(ln_b[0, 1]),
                       (nsc1, nsh1), w_rg[0], b_rg[0], w_re[0], b_re[0], w_gate[0], w_up[0], w_down[0])

    u = _glu_matmul(hn.reshape(n, d), w_pw1[0].astype(BF16), b_pw1[0])
    cv = _conv_ln_swish(u, s, w_dw[0], b_dw[0], conv_ln_g[0], conv_ln_b[0])
    y = _matmul(cv, w_pw2[0].astype(BF16))
    (x,) = _moe_layer(x, y, ng1, row(ln_g[1, 0]), row(ln_b[1, 0]), nsc2, nsh2, ng2, row(ln_g[1, 1]), row(ln_b[1, 1]),
                      None, w_rg[1], b_rg[1], w_re[1], b_re[1], w_gate[1], w_up[1], w_down[1])
    return x
```

```python
import functools

import jax
import jax.numpy as jnp
from jax import lax
from jax.experimental import pallas as pl
from jax.experimental.pallas import tpu as pltpu

F32 = jnp.float32
BF16 = jnp.bfloat16

HEAD_DIM = 128
GQA_GROUP = 4
GRID_W = 64
ROPE_THETA = 10000.0
N_FREQ = HEAD_DIM // 4
ATTN_SCALE = HEAD_DIM ** -0.5
LOG2_E = 1.4426950408889634
CONV_WIDTH = 31
CONV_PAD = CONV_WIDTH // 2
N_GROUPS = 4
EXPERTS_PER_GROUP = 8
N_EXPERTS = N_GROUPS * EXPERTS_PER_GROUP
TOP_K = 2
MOE_BLOCK = 256
LN_EPS = 1e-5
QK_EPS = 1e-6
DEPTH = 2
ALPHA = (2 * DEPTH) ** 0.25

LANES = 128
SUBLANES = 8
HALO = 16
ADA_ROWS = 8
VMEM_LIMIT = 56 * 1024 * 1024


def _params(sem, vmem=VMEM_LIMIT, flags=None):
    return pltpu.CompilerParams(dimension_semantics=sem, vmem_limit_bytes=vmem, flags=flags)


def _sigmoid(x):
    return 1.0 / (1.0 + jnp.exp(-x))


def _dot(a, b):
    return jnp.dot(a, b, preferred_element_type=F32)


def _ada_kernel(c_ref, w_ref, b_ref, o_ref):
    c = c_ref[...]
    s = (c * _sigmoid(c)).astype(BF16)
    o_ref[...] = _dot(s, w_ref[...].astype(BF16)) + b_ref[...]


def _ada_modulation(cond, w_ada, b_ada, tn=512):
    depth, d, n6 = w_ada.shape
    return pl.pallas_call(
        _ada_kernel,
        out_shape=jax.ShapeDtypeStruct((depth, ADA_ROWS, n6), F32),
        grid=(depth, n6 // tn),
        in_specs=[
            pl.BlockSpec((ADA_ROWS, d), lambda l, j: (0, 0)),
            pl.BlockSpec((None, d, tn), lambda l, j: (l, 0, j)),
            pl.BlockSpec((None, 1, tn), lambda l, j: (l, 0, j)),
        ],
        out_specs=pl.BlockSpec((None, ADA_ROWS, tn), lambda l, j: (l, 0, j)),
        compiler_params=_params(("parallel", "parallel")),
        name="ada_modulation",
    )(cond, w_ada, b_ada.reshape(depth, 1, n6))


def _modulate_kernel(x_ref, sc_ref, sh_ref, o_ref):
    o_ref[...] = (x_ref[...] * (1.0 + sc_ref[...]) + sh_ref[...]).astype(o_ref.dtype)


def _modulate(x, sc, sh, tm=256):
    b, s, d = x.shape
    tm = min(tm, s)
    return pl.pallas_call(
        _modulate_kernel,
        out_shape=jax.ShapeDtypeStruct((b, s, d), BF16),
        grid=(b, s // tm),
        in_specs=[
            pl.BlockSpec((None, tm, d), lambda bi, i: (bi, i, 0)),
            pl.BlockSpec((None, 1, d), lambda bi, i: (bi, 0, 0)),
            pl.BlockSpec((None, 1, d), lambda bi, i: (bi, 0, 0)),
        ],
        out_specs=pl.BlockSpec((None, tm, d), lambda bi, i: (bi, i, 0)),
        compiler_params=_params(("parallel", "parallel")),
        name="modulate",
    )(x, sc, sh)


def _qkv_kernel(x_ref, w_ref, gain_ref, scale_ref, cos_ref, sina_ref, sinb_ref, o_ref, *, first_v_tile, col_off):
    j = pl.program_id(1) + col_off
    y = _dot(x_ref[...], w_ref[...])
    tn = y.shape[1]

    @pl.when(j >= first_v_tile)
    def _():
        o_ref[...] = y.astype(o_ref.dtype)

    @pl.when(j < first_v_tile)
    def _():
        cos = cos_ref[...]
        sina = sina_ref[...]
        sinb = sinb_ref[...]
        for h in range(tn // HEAD_DIM):
            sl = slice(h * HEAD_DIM, (h + 1) * HEAD_DIM)
            yh = y[:, sl]
            ms = jnp.mean(yh * yh, axis=-1, keepdims=True)
            yn = yh * lax.rsqrt(ms + QK_EPS) * gain_ref[:, sl]
            rot = (yn * cos + pltpu.roll(yn, HEAD_DIM - N_FREQ, 1) * sina
                   + pltpu.roll(yn, N_FREQ, 1) * sinb)
            o_ref[:, sl] = (rot * scale_ref[:, sl]).astype(o_ref.dtype)


def _qkv_project(x2d, w, gain, scale, cos, sina, sinb, *, q_dim, kv_dim, col_off_cols, n_cols, tm, tn=512):
    m, d = x2d.shape
    tm = min(tm, m, cos.shape[0])
    tn = min(tn, kv_dim)
    pos_tiles = cos.shape[0] // tm
    col_off = col_off_cols // tn
    kern = functools.partial(_qkv_kernel, first_v_tile=(q_dim + kv_dim) // tn, col_off=col_off)
    return pl.pallas_call(
        kern,
        out_shape=jax.ShapeDtypeStruct((m, n_cols), BF16),
        grid=(m // tm, n_cols // tn),
        in_specs=[
            pl.BlockSpec((tm, d), lambda i, j: (i, 0)),
            pl.BlockSpec((d, tn), lambda i, j: (0, j + col_off)),
            pl.BlockSpec((1, tn), lambda i, j: (0, j + col_off)),
            pl.BlockSpec((1, tn), lambda i, j: (0, j + col_off)),
            pl.BlockSpec((tm, HEAD_DIM), lambda i, j: (i % pos_tiles, 0)),
            pl.BlockSpec((tm, HEAD_DIM), lambda i, j: (i % pos_tiles, 0)),
            pl.BlockSpec((tm, HEAD_DIM), lambda i, j: (i % pos_tiles, 0)),
        ],
        out_specs=pl.BlockSpec((tm, tn), lambda i, j: (i, j)),
        compiler_params=_params(("parallel", "parallel")),
        name="qkv_project",
    )(x2d, w, gain, scale, cos, sina, sinb)


def _attn_kernel(q_ref, k_ref, vt_ref, o_ref, qt_ref, s_bufs, p_bufs, al_bufs, m_ref, l_ref, acc_ref,
                 *, tq, tk, n_chunks, rg, unroll_pairs):
    for g in range(GQA_GROUP):
        qt_ref[:, g * tq:(g + 1) * tq] = q_ref[:, g * HEAD_DIM:(g + 1) * HEAD_DIM].T
    m_ref[...] = jnp.full(m_ref.shape, -jnp.inf, F32)
    l_ref[...] = jnp.zeros(l_ref.shape, F32)
    acc_ref[...] = jnp.zeros(acc_ref.shape, F32)

    def start(t):
        return pl.multiple_of(t * tk, tk)

    def scores(t, par):
        s_bufs[par] = _dot(k_ref[pl.ds(start(t), tk), :], qt_ref[...])

    def softmax(par):
        rows = s_bufs.shape[2]
        mx = s_bufs[par, 0:rg, :]
        for r in range(rg, tk, rg):
            mx = jnp.maximum(mx, s_bufs[par, r:r + rg, :])
        m_prev = m_ref[...]
        m_new = jnp.maximum(m_prev, jnp.max(mx, axis=0, keepdims=True))
        alpha = jnp.exp2(m_prev - m_new)
        psum = jnp.zeros((rg, rows), F32)
        for r in range(0, tk, rg):
            p = jnp.exp2(s_bufs[par, r:r + rg, :] - m_new)
            psum = psum + p
            p_bufs[par, r:r + rg, :] = p.astype(BF16)
        l_ref[...] = alpha * l_ref[...] + jnp.sum(psum, axis=0, keepdims=True)
        m_ref[...] = m_new
        al_bufs[par] = alpha

    def values(t, par):
        acc_ref[...] = al_bufs[par] * acc_ref[...] + _dot(vt_ref[:, pl.ds(start(t), tk)], p_bufs[par])

    def stage(t, par, do_scores=True, do_values=True):
        if do_scores:
            scores(t + 1, 1 - par)
        softmax(par)
        if do_values:
            values(t - 1, 1 - par)

    n = n_chunks
    scores(0, 0)
    stage(0, 0, do_scores=n > 1, do_values=False)
    n_mid = max(n - 2, 0)
    n_pairs = n_mid // 2

    def pair(i, carry):
        t = 2 * i + 1
        stage(t, 1)
        stage(t + 1, 0)
        return carry

    n_trips = n_pairs // unroll_pairs

    def trip(j, carry):
        for u in range(unroll_pairs):
            pair(j * unroll_pairs + u, carry)
        return carry

    if n_trips > 1:
        lax.fori_loop(0, n_trips, trip, 0)
    else:
        n_trips = 0
    for i in range(n_trips * unroll_pairs, n_pairs):
        pair(i, 0)
    t = 2 * n_pairs + 1
    if n_mid % 2:
        stage(t, 1)
        t += 1
    if n > 1:
        stage(t, t % 2, do_scores=False)
    values(n - 1, (n - 1) % 2)

    out = (acc_ref[...] / l_ref[...]).T
    for g in range(GQA_GROUP):
        o_ref[:, g * HEAD_DIM:(g + 1) * HEAD_DIM] = out[g * tq:(g + 1) * tq, :].astype(o_ref.dtype)


def _attention(q_src, k_all, vt_all, *, n_heads, n_kv, tq=256, tk=768):
    b, s, _ = q_src.shape
    sk = k_all.shape[1]
    tq = min(tq, s)
    assert sk % tk == 0 and s % tq == 0
    gw = GQA_GROUP * HEAD_DIM
    rows = GQA_GROUP * tq
    kern = functools.partial(_attn_kernel, tq=tq, tk=tk, n_chunks=sk // tk, rg=16, unroll_pairs=1)
    return pl.pallas_call(
        kern,
        out_shape=jax.ShapeDtypeStruct((b, s, n_heads * HEAD_DIM), BF16),
        grid=(b, n_kv, s // tq),
        in_specs=[
            pl.BlockSpec((None, tq, gw), lambda bi, h, i: (bi, i, h)),
            pl.BlockSpec((None, sk, HEAD_DIM), lambda bi, h, i: (bi, 0, h)),
            pl.BlockSpec((None, HEAD_DIM, sk), lambda bi, h, i: (bi, h, 0)),
        ],
        out_specs=pl.BlockSpec((None, tq, gw), lambda bi, h, i: (bi, i, h)),
        scratch_shapes=[
            pltpu.VMEM((HEAD_DIM, rows), BF16),
            pltpu.VMEM((2, tk, rows), F32),
            pltpu.VMEM((2, tk, rows), BF16),
            pltpu.VMEM((2, 1, rows), F32),
            pltpu.VMEM((1, rows), F32),
            pltpu.VMEM((1, rows), F32),
            pltpu.VMEM((HEAD_DIM, rows), F32),
        ],
        compiler_params=_params(("parallel", "parallel", "arbitrary")),
        name="attention",
    )(q_src, k_all, vt_all)


def _matmul_kernel(x_ref, w_ref, o_ref):
    o_ref[...] = _dot(x_ref[...], w_ref[...]).astype(o_ref.dtype)


def _matmul(x2d, w, tm=512, tn=1024):
    m, k = x2d.shape
    n = w.shape[1]
    tm, tn = min(tm, m), min(tn, n)
    return pl.pallas_call(
        _matmul_kernel,
        out_shape=jax.ShapeDtypeStruct((m, n), F32),
        grid=(m // tm, n // tn),
        in_specs=[pl.BlockSpec((tm, k), lambda i, j: (i, 0)), pl.BlockSpec((k, tn), lambda i, j: (0, j))],
        out_specs=pl.BlockSpec((tm, tn), lambda i, j: (i, j)),
        compiler_params=_params(("parallel", "parallel")),
        name="matmul",
    )(x2d, w)


def _glu_kernel(x_ref, wa_ref, wg_ref, ba_ref, bg_ref, o_ref):
    x = x_ref[...]
    a = _dot(x, wa_ref[...]) + ba_ref[...]
    g = _dot(x, wg_ref[...]) + bg_ref[...]
    o_ref[...] = a * _sigmoid(g)


def _glu_matmul(x2d, w, bias, tm=512, tn=512):
    m, k = x2d.shape
    n = w.shape[1] // 2
    tm, tn = min(tm, m), min(tn, n)
    nt = n // tn
    bias2 = bias.reshape(1, 2 * n)
    return pl.pallas_call(
        _glu_kernel,
        out_shape=jax.ShapeDtypeStruct((m, n), F32),
        grid=(m // tm, nt),
        in_specs=[
            pl.BlockSpec((tm, k), lambda i, j: (i, 0)),
            pl.BlockSpec((k, tn), lambda i, j: (0, j)),
            pl.BlockSpec((k, tn), lambda i, j: (0, j + nt)),
            pl.BlockSpec((1, tn), lambda i, j: (0, j)),
            pl.BlockSpec((1, tn), lambda i, j: (0, j + nt)),
        ],
        out_specs=pl.BlockSpec((tm, tn), lambda i, j: (i, j)),
        compiler_params=_params(("parallel", "parallel")),
        name="glu_matmul",
    )(x2d, w, w, bias2, bias2)


def _cast_kernel(*refs):
    k = len(refs) // 2
    for src, dst in zip(refs[:k], refs[k:]):
        dst[...] = src[...].astype(dst.dtype)


def _cast_expert_weights(layer, *weights, splits=2):
    n_e = weights[0].shape[1]
    in_specs, out_specs, out_shape = [], [], []
    for w in weights:
        _, _, r, c = w.shape
        in_specs.append(pl.BlockSpec((None, None, r // splits, c), lambda e, p: (layer, e, p, 0)))
        out_specs.append(pl.BlockSpec((None, r // splits, c), lambda e, p: (e, p, 0)))
        out_shape.append(jax.ShapeDtypeStruct((n_e, r, c), BF16))
    return pl.pallas_call(
        _cast_kernel,
        out_shape=tuple(out_shape),
        grid=(n_e, splits),
        in_specs=in_specs,
        out_specs=tuple(out_specs),
        compiler_params=_params(("parallel", "parallel")),
        name="cast_expert_weights",
    )(*weights)


def _layer_norm_rows(v, g, b):
    mu = jnp.mean(v, axis=-1, keepdims=True)
    cen = v - mu
    var = jnp.mean(cen * cen, axis=-1, keepdims=True)
    return cen * lax.rsqrt(var + LN_EPS) * g + b


def _pack_bf16_pairs(h):
    half = h.shape[1] // 2
    hb = h.astype(BF16).astype(F32)
    lo = pltpu.bitcast(hb[:, :half], jnp.uint32)
    hi = pltpu.bitcast(hb[:, half:], jnp.uint32)
    return (lo >> 16) | (hi & jnp.uint32(0xFFFF0000))


def _unpack_bf16_pairs(w):
    lo = pltpu.bitcast(w << 16, F32).astype(BF16)
    hi = pltpu.bitcast(w & jnp.uint32(0xFFFF0000), F32).astype(BF16)
    return lo, hi


def _route(logits):
    lane = lax.broadcasted_iota(jnp.int32, logits.shape, 1).astype(F32)
    neg = -jnp.inf
    big = float(LANES)
    is_g = lane < N_GROUPS
    gl = jnp.where(is_g, logits, neg)
    gmax = jnp.max(gl, axis=-1, keepdims=True)
    gidx = jnp.min(jnp.where(gl == gmax, lane, big), axis=-1, keepdims=True)
    gsum = jnp.sum(jnp.where(is_g, jnp.exp(gl - gmax), 0.0), axis=-1, keepdims=True)
    gw = 1.0 / gsum
    lo = N_GROUPS + EXPERTS_PER_GROUP * gidx
    el = jnp.where(lane >= lo, jnp.where(lane < lo + EXPERTS_PER_GROUP, logits, neg), neg)
    t1 = jnp.max(el, axis=-1, keepdims=True)
    i1 = jnp.min(jnp.where(el == t1, lane, big), axis=-1, keepdims=True)
    el2 = jnp.where(lane == i1, neg, el)
    t2 = jnp.max(el2, axis=-1, keepdims=True)
    i2 = jnp.min(jnp.where(el2 == t2, lane, big), axis=-1, keepdims=True)
    e = jnp.exp(t2 - t1)
    w1 = gw / (1.0 + e)
    w2 = gw * e / (1.0 + e)
    out = jnp.where(lane == 0.0, i1 - N_GROUPS, 0.0)
    out = jnp.where(lane == 1.0, i2 - N_GROUPS, out)
    out = jnp.where(lane == 2.0, w1, out)
    out = jnp.where(lane == 3.0, w2, out)
    return out


def _ln_router_kernel(x_ref, y_ref, g_ref, lng_ref, lnb_ref, sc_ref, sh_ref, wr_ref, br_ref,
                      xo_ref, hp_ref, route_ref):
    v = ALPHA * x_ref[...] + g_ref[...] * y_ref[...]
    xn = _layer_norm_rows(v, lng_ref[...], lnb_ref[...])
    xo_ref[...] = xn
    h = xn * (1.0 + sc_ref[...]) + sh_ref[...]
    hp_ref[...] = _pack_bf16_pairs(h)
    h_hi = h.astype(BF16)
    h_lo = (h - h_hi.astype(F32)).astype(BF16)
    wr = wr_ref[...]
    w_hi = wr.astype(BF16)
    w_lo = (wr - w_hi.astype(F32)).astype(BF16)
    logits = _dot(h_hi, w_hi) + _dot(h_lo, w_hi) + _dot(h_hi, w_lo) + br_ref[...]
    route_ref[...] = _route(logits)


def _ln_router(x, y, gate, lng, lnb, sc, sh, wr, br, tm=256):
    b, s, d = x.shape
    tm = min(tm, s)
    row = lambda: pl.BlockSpec((None, tm, d), lambda bi, i: (bi, i, 0))
    per_b = lambda: pl.BlockSpec((None, 1, d), lambda bi, i: (bi, 0, 0))
    const = lambda shape: pl.BlockSpec(shape, lambda bi, i: (0,) * len(shape))
    return pl.pallas_call(
        _ln_router_kernel,
        out_shape=(
            jax.ShapeDtypeStruct((b, s, d), F32),
            jax.ShapeDtypeStruct((b, s, d // 2), jnp.uint32),
            jax.ShapeDtypeStruct((b, s, LANES), F32),
        ),
        grid=(b, s // tm),
        in_specs=[row(), row(), per_b(), const((1, d)), const((1, d)), per_b(), per_b(),
                  const((d, LANES)), const((1, LANES))],
        out_specs=(
            row(),
            pl.BlockSpec((None, tm, d // 2), lambda bi, i: (bi, i, 0)),
            pl.BlockSpec((None, tm, LANES), lambda bi, i: (bi, i, 0)),
        ),
        compiler_params=_params(("parallel", "parallel")),
        name="ln_router",
    )(x, y, gate, lng, lnb, sc, sh, wr, br)


def _moe_kernel(be_ref, tok_ref, nu_ref, h_hbm, wg_ref, wu_ref, wd_ref, o_ref, xb_ref, sem):
    i = pl.program_id(0)
    n_used = nu_ref[0]
    half = xb_ref.shape[2]

    def row_copy(tok, slot, r):
        return pltpu.make_async_copy(h_hbm.at[pl.ds(tok, 1)], xb_ref.at[slot, pl.ds(r, 1)], sem.at[slot])

    def issue(blk, slot):
        base = blk * MOE_BLOCK
        for r in range(MOE_BLOCK):
            row_copy(tok_ref[base + r], slot, r).start()

    def wait_all(slot):
        for r in range(MOE_BLOCK):
            row_copy(0, slot, r).wait()

    @pl.when(i == 0)
    def _():
        issue(0, 0)

    @pl.when(i + 1 < n_used)
    def _():
        issue(i + 1, (i + 1) % 2)

    @pl.when(i < n_used)
    def _():
        slot = i % 2
        wait_all(slot)
        lo, hi = _unpack_bf16_pairs(xb_ref[slot])
        gate = _dot(lo, wg_ref[:half, :]) + _dot(hi, wg_ref[half:, :])
        up = _dot(lo, wu_ref[:half, :]) + _dot(hi, wu_ref[half:, :])
        act = (gate * _sigmoid(gate) * up).astype(BF16)
        o_ref[...] = _dot(act, wd_ref[...])

    @pl.when(i >= n_used)
    def _():
        o_ref[...] = jnp.zeros(o_ref.shape, o_ref.dtype)


def _moe_experts(hp2d, block_e, buf_tok, n_used, wg, wu, wd):
    n_blocks = block_e.shape[0]
    half = hp2d.shape[1]
    d = 2 * half
    de = wg.shape[2]
    grid_spec = pltpu.PrefetchScalarGridSpec(
        num_scalar_prefetch=3,
        grid=(n_blocks,),
        in_specs=[
            pl.BlockSpec(memory_space=pl.ANY),
            pl.BlockSpec((None, d, de), lambda i, be, tok, nu: (be[i], 0, 0)),
            pl.BlockSpec((None, d, de), lambda i, be, tok, nu: (be[i], 0, 0)),
            pl.BlockSpec((None, de, d), lambda i, be, tok, nu: (be[i], 0, 0)),
        ],
        out_specs=pl.BlockSpec((MOE_BLOCK, d), lambda i, be, tok, nu: (i, 0)),
        scratch_shapes=[pltpu.VMEM((2, MOE_BLOCK, half), jnp.uint32), pltpu.SemaphoreType.DMA((2,))],
    )
    return pl.pallas_call(
        _moe_kernel,
        out_shape=jax.ShapeDtypeStruct((n_blocks * MOE_BLOCK, d), F32),
        grid_spec=grid_spec,
        compiler_params=_params(("arbitrary",)),
        name="moe_experts",
    )(block_e, buf_tok, n_used, hp2d, wg, wu, wd)


def _combine_kernel(pos_ref, ys_hbm, route_ref, x_ref, g_ref, lng_ref, lnb_ref, *rest, tm, with_h):
    if with_h:
        sc_ref, sh_ref, xo_ref, ho_ref, yb_ref, sem = rest
    else:
        xo_ref, yb_ref, sem = rest
    nt = pl.num_programs(1)
    step = pl.program_id(0) * nt + pl.program_id(1)
    total = pl.num_programs(0) * nt

    def row_copy(p, slot, k, r):
        return pltpu.make_async_copy(ys_hbm.at[pl.ds(p, 1)], yb_ref.at[slot, k, pl.ds(r, 1)], sem.at[slot])

    def issue(st, slot):
        base = TOP_K * st * tm
        for r in range(tm):
            for k in range(TOP_K):
                row_copy(pos_ref[base + TOP_K * r + k], slot, k, r).start()

    def wait_all(slot):
        for r in range(tm):
            for k in range(TOP_K):
                row_copy(0, slot, k, r).wait()

    @pl.when(step == 0)
    def _():
        issue(0, 0)

    @pl.when(step + 1 < total)
    def _():
        issue(step + 1, (step + 1) % 2)

    slot = step % 2
    wait_all(slot)
    route = route_ref[...]
    f = route[:, 2:3] * yb_ref[slot, 0] + route[:, 3:4] * yb_ref[slot, 1]
    v = ALPHA * x_ref[...] + g_ref[...] * f
    xn = _layer_norm_rows(v, lng_ref[...], lnb_ref[...])
    xo_ref[...] = xn
    if with_h:
        ho_ref[...] = (xn * (1.0 + sc_ref[...]) + sh_ref[...]).astype(ho_ref.dtype)


def _moe_combine(ys, pos, route, x, gate, lng, lnb, next_mod=None, tm=256):
    b, s, d = x.shape
    tm = min(tm, s)
    with_h = next_mod is not None
    row = lambda w: pl.BlockSpec((None, tm, w), lambda bi, i, pos: (bi, i, 0))
    per_b = lambda: pl.BlockSpec((None, 1, d), lambda bi, i, pos: (bi, 0, 0))
    const = lambda: pl.BlockSpec((1, d), lambda bi, i, pos: (0, 0))
    in_specs = [pl.BlockSpec(memory_space=pl.ANY), row(LANES), row(d), per_b(), const(), const()]
    out_specs, out_shape = [row(d)], [jax.ShapeDtypeStruct((b, s, d), F32)]
    args = [pos, ys, route, x, gate, lng, lnb]
    if with_h:
        in_specs += [per_b(), per_b()]
        out_specs.append(row(d))
        out_shape.append(jax.ShapeDtypeStruct((b, s, d), BF16))
        args += list(next_mod)
    grid_spec = pltpu.PrefetchScalarGridSpec(
        num_scalar_prefetch=1,
        grid=(b, s // tm),
        in_specs=in_specs,
        out_specs=tuple(out_specs),
        scratch_shapes=[pltpu.VMEM((2, TOP_K, tm, d), F32), pltpu.SemaphoreType.DMA((2,))],
    )
    kern = functools.partial(_combine_kernel, tm=tm, with_h=with_h)
    return pl.pallas_call(
        kern,
        out_shape=tuple(out_shape),
        grid_spec=grid_spec,
        compiler_params=_params(("arbitrary", "arbitrary")),
        name="moe_combine",
    )(*args)


def _conv_kernel(prev_ref, cur_ref, next_ref, w_ref, b_ref, g_ref, beta_ref, o_ref, buf_ref, acc_ref, sh_ref,
                 *, ts, tiles_per_seq, rs, cw):
    win_rows = sh_ref.shape[2]
    i = pl.program_id(0)
    first = (i % tiles_per_seq) == 0
    last = (i % tiles_per_seq) == tiles_per_seq - 1
    buf_ref[0:HALO, :] = jnp.where(first, 0.0, prev_ref[...])
    buf_ref[HALO:HALO + ts, :] = cur_ref[...]
    buf_ref[HALO + ts:, :] = jnp.where(last, 0.0, next_ref[...])
    d = cur_ref.shape[1]
    off = HALO - CONV_PAD

    def chunk(ci, carry):
        c0 = pl.multiple_of(ci * cw, cw)
        slot = ci % 2
        for r0 in range(0, ts, rs):
            for phase in range(1, SUBLANES):
                sh_ref[slot, phase] = buf_ref[r0 + phase:r0 + phase + win_rows, pl.ds(c0, cw)]
            acc = jnp.zeros((rs, cw), F32)
            for k in range(CONV_WIDTH):
                j = k + off
                phase, base = j % SUBLANES, j - j % SUBLANES
                if phase == 0:
                    tap = buf_ref[r0 + j:r0 + j + rs, pl.ds(c0, cw)]
                else:
                    tap = sh_ref[slot, phase, base:base + rs, :]
                acc = acc + w_ref[k:k + 1, pl.ds(c0, cw)] * tap
            acc_ref[r0:r0 + rs, pl.ds(c0, cw)] = acc
        return carry

    lax.fori_loop(0, d // cw, chunk, 0)
    u = acc_ref[...] + b_ref[...]
    z = _layer_norm_rows(u, g_ref[...], beta_ref[...])
    o_ref[...] = (z * _sigmoid(z)).astype(o_ref.dtype)


def _conv_ln_swish(u2d, seq_len, w_dw, b_dw, ln_g, ln_b, ts=256, rs=128, cw=128):
    n, d = u2d.shape
    ts = min(ts, seq_len)
    rs = min(rs, ts)
    tiles_per_seq = seq_len // ts
    hb = ts // HALO
    n_halo = n // HALO
    w_pad = jnp.zeros((CONV_WIDTH + 1, d), F32).at[:CONV_WIDTH].set(w_dw)
    last_off = HALO - CONV_PAD + CONV_WIDTH - 1
    max_base = last_off - last_off % SUBLANES
    assert max_base + SUBLANES <= 2 * HALO
    kern = functools.partial(_conv_kernel, ts=ts, tiles_per_seq=tiles_per_seq, rs=rs, cw=cw)
    const = lambda r: pl.BlockSpec((r, d), lambda i: (0, 0))
    return pl.pallas_call(
        kern,
        out_shape=jax.ShapeDtypeStruct((n, d), BF16),
        grid=(n // ts,),
        in_specs=[
            pl.BlockSpec((HALO, d), lambda i: (jnp.maximum(i * hb - 1, 0), 0)),
            pl.BlockSpec((ts, d), lambda i: (i, 0)),
            pl.BlockSpec((HALO, d), lambda i: (jnp.minimum((i + 1) * hb, n_halo - 1), 0)),
            const(CONV_WIDTH + 1), const(1), const(1), const(1),
        ],
        out_specs=pl.BlockSpec((ts, d), lambda i: (i, 0)),
        scratch_shapes=[pltpu.VMEM((ts + 2 * HALO, d), F32), pltpu.VMEM((ts, d), F32),
                        pltpu.VMEM((2, SUBLANES, rs + max_base, cw), F32)],
        compiler_params=_params(("parallel",)),
        name="conv_ln_swish",
    )(u2d, u2d, u2d, w_pad, b_dw.reshape(1, d), ln_g.reshape(1, d), ln_b.reshape(1, d))


def _rope_tables(seq_len):
    inv = ROPE_THETA ** (-jnp.arange(N_FREQ, dtype=F32) / N_FREQ)
    n_rows = seq_len // GRID_W
    row = jnp.repeat(jnp.arange(n_rows, dtype=F32), GRID_W)
    col = jnp.tile(jnp.arange(GRID_W, dtype=F32), n_rows)
    ang_r, ang_c = row[:, None] * inv, col[:, None] * inv
    cr, sr, cc, sc = jnp.cos(ang_r), jnp.sin(ang_r), jnp.cos(ang_c), jnp.sin(ang_c)
    z = jnp.zeros_like(sr)
    cos = jnp.concatenate([cr, cr, cc, cc], axis=-1)
    sina = jnp.concatenate([-sr, z, -sc, z], axis=-1)
    sinb = jnp.concatenate([z, sr, z, sc], axis=-1)
    return cos, sina, sinb


def _routing_plan(route2d):
    n = route2d.shape[0]
    a = n * TOP_K
    eid_f = route2d[:, :TOP_K].astype(jnp.int32).reshape(a)
    experts = jnp.arange(N_EXPERTS, dtype=jnp.int32)
    counts = jnp.sum(eid_f[:, None] == experts[None, :], axis=0, dtype=jnp.int32)
    pcounts = (counts + MOE_BLOCK - 1) // MOE_BLOCK * MOE_BLOCK
    starts = jnp.cumsum(counts) - counts
    pends = jnp.cumsum(pcounts)
    pstarts = pends - pcounts
    order = jnp.argsort(eid_f)
    rank = jnp.argsort(order)
    pos = (pstarts[eid_f] + rank - starts[eid_f]).astype(jnp.int32)
    n_blocks = -(-a // MOE_BLOCK) + N_EXPERTS
    blk_start = jnp.arange(n_blocks, dtype=jnp.int32) * MOE_BLOCK
    block_e = jnp.minimum(jnp.sum(pends[None, :] <= blk_start[:, None], axis=1, dtype=jnp.int32), N_EXPERTS - 1)
    e_slot = jnp.repeat(block_e, MOE_BLOCK)
    j = jnp.arange(n_blocks * MOE_BLOCK, dtype=jnp.int32) - pstarts[e_slot]
    src = jnp.clip(starts[e_slot] + j, 0, a - 1)
    buf_tok = jnp.where(j < counts[e_slot], order[src] // TOP_K, 0).astype(jnp.int32)
    n_used = (pends[-1:] // MOE_BLOCK).astype(jnp.int32)
    return block_e, buf_tok, n_used, pos


def _moe_layer(x, y, gate1, lng1, lnb1, sc2, sh2, gate2, lng2, lnb2, next_mod,
               w_rg, b_rg, w_re, b_re, expert_weights):
    b, s, d = x.shape
    n = b * s
    pad = LANES - N_GROUPS - N_EXPERTS
    wr = jnp.concatenate([w_rg, w_re, jnp.zeros((d, pad), F32)], axis=1)
    br = jnp.concatenate([b_rg, b_re, jnp.zeros((pad,), F32)]).reshape(1, LANES)
    x1, hp, route = _ln_router(x, y.reshape(b, s, d), gate1, lng1, lnb1, sc2, sh2, wr, br)
    block_e, buf_tok, n_used, pos = _routing_plan(route.reshape(n, LANES))
    ys = _moe_experts(hp.reshape(n, d // 2), block_e, buf_tok, n_used, *expert_weights)
    return _moe_combine(ys, pos, route, x1, gate2, lng2, lnb2, next_mod)


def kernel(x, c, ctx, c_ctx, w_ada, b_ada, ln_g, ln_b, w_qkv, q_gain, k_gain, w_o, w_pw1, b_pw1, w_dw, b_dw,
           conv_ln_g, conv_ln_b, w_pw2, w_rg, b_rg, w_re, b_re, w_gate, w_up, w_down):
    b, s, d = x.shape
    cl = ctx.shape[1]
    n = b * s
    n_heads = d // HEAD_DIM
    n_kv = n_heads // GQA_GROUP
    q_dim, kv_dim = n_heads * HEAD_DIM, n_kv * HEAD_DIM
    assert w_ada.shape[0] == DEPTH and b + 1 <= ADA_ROWS

    cond = jnp.zeros((ADA_ROWS, d), F32).at[:b].set(c).at[b].set(c_ctx)
    mods = _ada_modulation(cond, w_ada, b_ada)

    def mod_chunks(layer, rows):
        m = mods[layer, rows].reshape(-1, 6, d)
        return [m[:, k][:, None, :] for k in range(6)]

    row = lambda v: v.reshape(1, d)

    sh1, sc1, g1, sh2, sc2, g2 = mod_chunks(0, slice(0, b))
    csh1, csc1 = [jnp.broadcast_to(m, (b, 1, d)) for m in mod_chunks(0, slice(b, b + 1))[:2]]
    nsh1, nsc1, ng1, nsh2, nsc2, ng2 = mod_chunks(1, slice(0, b))

    h = _modulate(x, sc1, sh1)
    hc = _modulate(ctx, csc1, csh1)
    wqkv = w_qkv[0].astype(BF16)
    ones_kv = jnp.ones((kv_dim,), F32)
    gain = jnp.concatenate([jnp.tile(q_gain[0], n_heads), jnp.tile(k_gain[0], n_kv), ones_kv]).reshape(1, -1)
    scale = jnp.concatenate([jnp.full((q_dim,), ATTN_SCALE * LOG2_E, F32), ones_kv, ones_kv]).reshape(1, -1)
    cos, sina, sinb = _rope_tables(s)
    qkv = _qkv_project(h.reshape(n, d), wqkv, gain, scale, cos, sina, sinb, q_dim=q_dim, kv_dim=kv_dim,
                       col_off_cols=0, n_cols=q_dim + 2 * kv_dim, tm=1024)
    one_c, zero_c = jnp.ones((cl, HEAD_DIM), F32), jnp.zeros((cl, HEAD_DIM), F32)
    kvc = _qkv_project(hc.reshape(b * cl, d), wqkv, gain, scale, one_c, zero_c, zero_c, q_dim=q_dim, kv_dim=kv_dim,
                       col_off_cols=q_dim, n_cols=2 * kv_dim, tm=cl)
    qkv, kvc = qkv.reshape(b, s, -1), kvc.reshape(b, cl, -1)
    k_all = jnp.concatenate([qkv[:, :, q_dim:q_dim + kv_dim], kvc[:, :, :kv_dim]], axis=1)
    v_all = jnp.concatenate([qkv[:, :, q_dim + kv_dim:], kvc[:, :, kv_dim:]], axis=1)
    o = _attention(qkv, k_all, jnp.swapaxes(v_all, 1, 2), n_heads=n_heads, n_kv=n_kv)
    y = _matmul(o.reshape(n, d), w_o[0].astype(BF16))
    x, hn = _moe_layer(x, y, g1, row(ln_g[0, 0]), row(ln_b[0, 0]), sc2, sh2, g2, row(ln_g[0, 1]), row(ln_b[0, 1]),
                       (nsc1, nsh1), w_rg[0], b_rg[0], w_re[0], b_re[0],
                       _cast_expert_weights(0, w_gate, w_up, w_down))

    u = _glu_matmul(hn.reshape(n, d), w_pw1[0].astype(BF16), b_pw1[0])
    cv = _conv_ln_swish(u, s, w_dw[0], b_dw[0], conv_ln_g[0], conv_ln_b[0])
    y = _matmul(cv, w_pw2[0].astype(BF16))
    (x,) = _moe_layer(x, y, ng1, row(ln_g[1, 0]), row(ln_b[1, 0]), nsc2, nsh2, ng2, row(ln_g[1, 1]), row(ln_b[1, 1]),
                      None, w_rg[1], b_rg[1], w_re[1], b_re[1], _cast_expert_weights(1, w_gate, w_up, w_down))
    return x
```

```python
import functools
import math

import jax
import jax.numpy as jnp
from jax import lax
from jax.experimental import pallas as pl
from jax.experimental.pallas import tpu as pltpu

F32 = jnp.float32
BF16 = jnp.bfloat16

HEAD_DIM = 128
GQA_GROUP = 4
GRID_W = 64
ROPE_THETA = 10000.0
N_FREQ = HEAD_DIM // 4
ATTN_SCALE = HEAD_DIM ** -0.5
LOG2_E = 1.4426950408889634
CONV_WIDTH = 31
CONV_PAD = CONV_WIDTH // 2
N_GROUPS = 4
EXPERTS_PER_GROUP = 8
N_EXPERTS = N_GROUPS * EXPERTS_PER_GROUP
TOP_K = 2
MOE_BLOCK = 256
LN_EPS = 1e-5
QK_EPS = 1e-6
DEPTH = 2
ALPHA = (2 * DEPTH) ** 0.25

V_EXTRA = 16
V_ROWS = HEAD_DIM + V_EXTRA
LANES = 128
SUBLANES = 8
HALO = 16
ADA_ROWS = 8
VMEM_LIMIT = 56 * 1024 * 1024


def _params(sem, vmem=VMEM_LIMIT, flags=None):
    return pltpu.CompilerParams(dimension_semantics=sem, vmem_limit_bytes=vmem, flags=flags)


def _sigmoid(x):
    return 1.0 / (1.0 + jnp.exp(-x))


def _dot(a, b):
    return jnp.dot(a, b, preferred_element_type=F32)


def _ada_kernel(c_ref, w_ref, b_ref, o_ref):
    c = c_ref[...]
    s = (c * _sigmoid(c)).astype(BF16)
    o_ref[...] = _dot(s, w_ref[...].astype(BF16)) + b_ref[...]


def _ada_modulation(cond, w_ada, b_ada, tn=512):
    depth, d, n6 = w_ada.shape
    return pl.pallas_call(
        _ada_kernel,
        out_shape=jax.ShapeDtypeStruct((depth, ADA_ROWS, n6), F32),
        grid=(depth, n6 // tn),
        in_specs=[
            pl.BlockSpec((ADA_ROWS, d), lambda l, j: (0, 0)),
            pl.BlockSpec((None, d, tn), lambda l, j: (l, 0, j)),
            pl.BlockSpec((None, 1, tn), lambda l, j: (l, 0, j)),
        ],
        out_specs=pl.BlockSpec((None, ADA_ROWS, tn), lambda l, j: (l, 0, j)),
        compiler_params=_params(("parallel", "parallel")),
        name="ada_modulation",
    )(cond, w_ada, b_ada.reshape(depth, 1, n6))


def _modulate_kernel(x_ref, sc_ref, sh_ref, o_ref):
    o_ref[...] = (x_ref[...] * (1.0 + sc_ref[...]) + sh_ref[...]).astype(o_ref.dtype)


def _modulate(x, sc, sh, tm=256):
    b, s, d = x.shape
    tm = min(tm, s)
    return pl.pallas_call(
        _modulate_kernel,
        out_shape=jax.ShapeDtypeStruct((b, s, d), BF16),
        grid=(b, s // tm),
        in_specs=[
            pl.BlockSpec((None, tm, d), lambda bi, i: (bi, i, 0)),
            pl.BlockSpec((None, 1, d), lambda bi, i: (bi, 0, 0)),
            pl.BlockSpec((None, 1, d), lambda bi, i: (bi, 0, 0)),
        ],
        out_specs=pl.BlockSpec((None, tm, d), lambda bi, i: (bi, i, 0)),
        compiler_params=_params(("parallel", "parallel")),
        name="modulate",
    )(x, sc, sh)


def _qk_kernel(x_ref, w_ref, gain_ref, scale_ref, cos_ref, sin_ref, o_ref, *, sub):
    x = x_ref[...]
    cos = cos_ref[...]
    sin = sin_ref[...]
    for c in range(0, o_ref.shape[1], sub):
        y = _dot(x, w_ref[:, c:c + sub])
        for h in range(sub // HEAD_DIM):
            sl = slice(c + h * HEAD_DIM, c + (h + 1) * HEAD_DIM)
            yh = y[:, h * HEAD_DIM:(h + 1) * HEAD_DIM]
            ms = jnp.mean(yh * yh, axis=-1, keepdims=True)
            yn = yh * lax.rsqrt(ms + QK_EPS) * gain_ref[:, sl]
            rot = yn * cos + pltpu.roll(yn, HEAD_DIM // 2, 1) * sin
            o_ref[:, sl] = (rot * scale_ref[:, sl]).astype(o_ref.dtype)


def _pair_halves(a):
    lead = a.shape[:-1]
    a = a.reshape(lead + (a.shape[-1] // HEAD_DIM, 2, 2, N_FREQ))
    return jnp.swapaxes(a, -2, -3).reshape(lead + (-1,))


def _qk_project(x2d, w, gain, scale, cos, sin, *, col_off_cols, n_cols, tm, tn=1024, sub=256):
    m, d = x2d.shape
    tm = min(tm, m, cos.shape[0])
    tn = math.gcd(tn, n_cols, col_off_cols)
    sub = min(sub, tn)
    pos_tiles = cos.shape[0] // tm
    col_off = col_off_cols // tn
    kern = functools.partial(_qk_kernel, sub=sub)
    return pl.pallas_call(
        kern,
        out_shape=jax.ShapeDtypeStruct((m, n_cols), BF16),
        grid=(m // tm, n_cols // tn),
        in_specs=[
            pl.BlockSpec((tm, d), lambda i, j: (i, 0)),
            pl.BlockSpec((d, tn), lambda i, j: (0, j + col_off)),
            pl.BlockSpec((1, tn), lambda i, j: (0, j + col_off)),
            pl.BlockSpec((1, tn), lambda i, j: (0, j + col_off)),
            pl.BlockSpec((tm, HEAD_DIM), lambda i, j: (i % pos_tiles, 0)),
            pl.BlockSpec((tm, HEAD_DIM), lambda i, j: (i % pos_tiles, 0)),
        ],
        out_specs=pl.BlockSpec((tm, tn), lambda i, j: (i, j)),
        compiler_params=_params(("parallel", "parallel")),
        name="qk_project",
    )(x2d, w, gain, scale, cos, sin)


def _attn_kernel(q_ref, k_ref, vt_ref, o_ref, qt_ref, s_bufs, p_bufs, al_bufs, m_ref, acc_ref,
                 *, tq, tk, n_chunks, rg, unroll_pairs):
    for g in range(GQA_GROUP):
        qt_ref[:, g * tq:(g + 1) * tq] = q_ref[:, g * HEAD_DIM:(g + 1) * HEAD_DIM].T
    m_ref[...] = jnp.full(m_ref.shape, -jnp.inf, F32)
    acc_ref[...] = jnp.zeros(acc_ref.shape, F32)

    def start(t):
        return pl.multiple_of(t * tk, tk)

    def scores(t, par):
        s_bufs[par] = _dot(k_ref[pl.ds(start(t), tk), :], qt_ref[...])

    def softmax(par):
        mx = s_bufs[par, 0:rg, :]
        for r in range(rg, tk, rg):
            mx = jnp.maximum(mx, s_bufs[par, r:r + rg, :])
        m_prev = m_ref[...]
        m_new = jnp.maximum(m_prev, jnp.max(mx, axis=0, keepdims=True))
        for r in range(0, tk, rg):
            p_bufs[par, r:r + rg, :] = jnp.exp2(s_bufs[par, r:r + rg, :] - m_new).astype(BF16)
        m_ref[...] = m_new
        al_bufs[par] = jnp.exp2(m_prev - m_new)

    def values(t, par):
        acc_ref[...] = al_bufs[par] * acc_ref[...] + _dot(vt_ref[:, pl.ds(start(t), tk)], p_bufs[par])

    def stage(t, par, do_scores=True, do_values=True):
        if do_scores:
            scores(t + 1, 1 - par)
        softmax(par)
        if do_values:
            values(t - 1, 1 - par)

    n = n_chunks
    scores(0, 0)
    stage(0, 0, do_scores=n > 1, do_values=False)
    n_mid = max(n - 2, 0)
    n_pairs = n_mid // 2

    def pair(i, carry):
        t = 2 * i + 1
        stage(t, 1)
        stage(t + 1, 0)
        return carry

    n_trips = n_pairs // unroll_pairs

    def trip(j, carry):
        for u in range(unroll_pairs):
            pair(j * unroll_pairs + u, carry)
        return carry

    if n_trips > 1:
        lax.fori_loop(0, n_trips, trip, 0)
    else:
        n_trips = 0
    for i in range(n_trips * unroll_pairs, n_pairs):
        pair(i, 0)
    t = 2 * n_pairs + 1
    if n_mid % 2:
        stage(t, 1)
        t += 1
    if n > 1:
        stage(t, t % 2, do_scores=False)
    values(n - 1, (n - 1) % 2)

    out = (acc_ref[:HEAD_DIM, :] / acc_ref[HEAD_DIM:HEAD_DIM + 1, :]).T
    for g in range(GQA_GROUP):
        o_ref[:, g * HEAD_DIM:(g + 1) * HEAD_DIM] = out[g * tq:(g + 1) * tq, :].astype(o_ref.dtype)


def _attention(q_src, k_all, vt_all, *, n_heads, n_kv, tq=256, tk=768):
    b, s, _ = q_src.shape
    sk = k_all.shape[1]
    tq = min(tq, s)
    assert sk % tk == 0 and s % tq == 0
    gw = GQA_GROUP * HEAD_DIM
    rows = GQA_GROUP * tq
    kern = functools.partial(_attn_kernel, tq=tq, tk=tk, n_chunks=sk // tk, rg=16, unroll_pairs=4)
    return pl.pallas_call(
        kern,
        out_shape=jax.ShapeDtypeStruct((b, s, n_heads * HEAD_DIM), BF16),
        grid=(b, n_kv, s // tq),
        in_specs=[
            pl.BlockSpec((None, tq, gw), lambda bi, h, i: (bi, i, h)),
            pl.BlockSpec((None, sk, HEAD_DIM), lambda bi, h, i: (bi, 0, h)),
            pl.BlockSpec((None, V_ROWS, sk), lambda bi, h, i: (bi, h, 0)),
        ],
        out_specs=pl.BlockSpec((None, tq, gw), lambda bi, h, i: (bi, i, h)),
        scratch_shapes=[
            pltpu.VMEM((HEAD_DIM, rows), BF16),
            pltpu.VMEM((2, tk, rows), F32),
            pltpu.VMEM((2, tk, rows), BF16),
            pltpu.VMEM((2, 1, rows), F32),
            pltpu.VMEM((1, rows), F32),
            pltpu.VMEM((V_ROWS, rows), F32),
        ],
        compiler_params=_params(("parallel", "parallel", "arbitrary")),
        name="attention",
    )(q_src, k_all, vt_all)


def _matmul_kernel(x_ref, w_ref, o_ref):
    o_ref[...] = _dot(x_ref[...], w_ref[...]).astype(o_ref.dtype)


def _matmul(x2d, w, tm=512, tn=1024, col_off_cols=0, n_cols=None, out_dtype=F32):
    m, k = x2d.shape
    n = w.shape[1] if n_cols is None else n_cols
    tm, tn = min(tm, m), math.gcd(tn, n, col_off_cols)
    col_off = col_off_cols // tn
    return pl.pallas_call(
        _matmul_kernel,
        out_shape=jax.ShapeDtypeStruct((m, n), out_dtype),
        grid=(m // tm, n // tn),
        in_specs=[pl.BlockSpec((tm, k), lambda i, j: (i, 0)), pl.BlockSpec((k, tn), lambda i, j: (0, j + col_off))],
        out_specs=pl.BlockSpec((tm, tn), lambda i, j: (i, j)),
        compiler_params=_params(("parallel", "parallel")),
        name="matmul",
    )(x2d, w)


def _glu_kernel(x_ref, wa_ref, wg_ref, ba_ref, bg_ref, o_ref):
    x = x_ref[...]
    a = _dot(x, wa_ref[...]) + ba_ref[...]
    g = _dot(x, wg_ref[...]) + bg_ref[...]
    o_ref[...] = a * _sigmoid(g)


def _glu_matmul(x2d, w, bias, tm=512, tn=512):
    m, k = x2d.shape
    n = w.shape[1] // 2
    tm, tn = min(tm, m), min(tn, n)
    nt = n // tn
    bias2 = bias.reshape(1, 2 * n)
    return pl.pallas_call(
        _glu_kernel,
        out_shape=jax.ShapeDtypeStruct((m, n), F32),
        grid=(m // tm, nt),
        in_specs=[
            pl.BlockSpec((tm, k), lambda i, j: (i, 0)),
            pl.BlockSpec((k, tn), lambda i, j: (0, j)),
            pl.BlockSpec((k, tn), lambda i, j: (0, j + nt)),
            pl.BlockSpec((1, tn), lambda i, j: (0, j)),
            pl.BlockSpec((1, tn), lambda i, j: (0, j + nt)),
        ],
        out_specs=pl.BlockSpec((tm, tn), lambda i, j: (i, j)),
        compiler_params=_params(("parallel", "parallel")),
        name="glu_matmul",
    )(x2d, w, w, bias2, bias2)


def _layer_norm_rows(v, g, b):
    mu = jnp.mean(v, axis=-1, keepdims=True)
    cen = v - mu
    var = jnp.mean(cen * cen, axis=-1, keepdims=True)
    return cen * lax.rsqrt(var + LN_EPS) * g + b


def _pack_bf16_pairs(h):
    half = h.shape[1] // 2
    hb = h.astype(BF16).astype(F32)
    lo = pltpu.bitcast(hb[:, :half], jnp.uint32)
    hi = pltpu.bitcast(hb[:, half:], jnp.uint32)
    return (lo >> 16) | (hi & jnp.uint32(0xFFFF0000))


def _unpack_bf16_pairs(w):
    lo = pltpu.bitcast(w << 16, F32).astype(BF16)
    hi = pltpu.bitcast(w & jnp.uint32(0xFFFF0000), F32).astype(BF16)
    return lo, hi


def _route(logits):
    lane = lax.broadcasted_iota(jnp.int32, logits.shape, 1).astype(F32)
    neg = -jnp.inf
    big = float(LANES)
    is_g = lane < N_GROUPS
    gl = jnp.where(is_g, logits, neg)
    gmax = jnp.max(gl, axis=-1, keepdims=True)
    gidx = jnp.min(jnp.where(gl == gmax, lane, big), axis=-1, keepdims=True)
    gsum = jnp.sum(jnp.where(is_g, jnp.exp(gl - gmax), 0.0), axis=-1, keepdims=True)
    gw = 1.0 / gsum
    lo = N_GROUPS + EXPERTS_PER_GROUP * gidx
    el = jnp.where(lane >= lo, jnp.where(lane < lo + EXPERTS_PER_GROUP, logits, neg), neg)
    t1 = jnp.max(el, axis=-1, keepdims=True)
    i1 = jnp.min(jnp.where(el == t1, lane, big), axis=-1, keepdims=True)
    el2 = jnp.where(lane == i1, neg, el)
    t2 = jnp.max(el2, axis=-1, keepdims=True)
    i2 = jnp.min(jnp.where(el2 == t2, lane, big), axis=-1, keepdims=True)
    e = jnp.exp(t2 - t1)
    w1 = gw / (1.0 + e)
    w2 = gw * e / (1.0 + e)
    out = jnp.where(lane == 0.0, i1 - N_GROUPS, 0.0)
    out = jnp.where(lane == 1.0, i2 - N_GROUPS, out)
    out = jnp.where(lane == 2.0, w1, out)
    out = jnp.where(lane == 3.0, w2, out)
    return out


def _ln_router_kernel(x_ref, y_ref, g_ref, lng_ref, lnb_ref, sc_ref, sh_ref, wr_ref, br_ref,
                      xo_ref, hp_ref, route_ref):
    v = ALPHA * x_ref[...] + g_ref[...] * y_ref[...]
    xn = _layer_norm_rows(v, lng_ref[...], lnb_ref[...])
    xo_ref[...] = xn
    h = xn * (1.0 + sc_ref[...]) + sh_ref[...]
    hp_ref[...] = _pack_bf16_pairs(h)
    h_hi = h.astype(BF16)
    h_lo = (h - h_hi.astype(F32)).astype(BF16)
    wr = wr_ref[...]
    w_hi = wr.astype(BF16)
    w_lo = (wr - w_hi.astype(F32)).astype(BF16)
    logits = _dot(h_hi, w_hi) + _dot(h_lo, w_hi) + _dot(h_hi, w_lo) + br_ref[...]
    route_ref[...] = _route(logits)


def _ln_router(x, y, gate, lng, lnb, sc, sh, wr, br, tm=256):
    b, s, d = x.shape
    tm = min(tm, s)
    row = lambda: pl.BlockSpec((None, tm, d), lambda bi, i: (bi, i, 0))
    per_b = lambda: pl.BlockSpec((None, 1, d), lambda bi, i: (bi, 0, 0))
    const = lambda shape: pl.BlockSpec(shape, lambda bi, i: (0,) * len(shape))
    return pl.pallas_call(
        _ln_router_kernel,
        out_shape=(
            jax.ShapeDtypeStruct((b, s, d), F32),
            jax.ShapeDtypeStruct((b, s, d // 2), jnp.uint32),
            jax.ShapeDtypeStruct((b, s, LANES), F32),
        ),
        grid=(b, s // tm),
        in_specs=[row(), row(), per_b(), const((1, d)), const((1, d)), per_b(), per_b(),
                  const((d, LANES)), const((1, LANES))],
        out_specs=(
            row(),
            pl.BlockSpec((None, tm, d // 2), lambda bi, i: (bi, i, 0)),
            pl.BlockSpec((None, tm, LANES), lambda bi, i: (bi, i, 0)),
        ),
        compiler_params=_params(("parallel", "parallel")),
        name="ln_router",
    )(x, y, gate, lng, lnb, sc, sh, wr, br)


def _moe_kernel(be_ref, tok_ref, nu_ref, first_ref, next_ref, h_hbm, wg_hbm, wu_hbm, wd_hbm, o_ref,
                xb_ref, sem, sg_ref, su_ref, sd_ref, wsem, wg_ref, wu_ref, wd_ref, *, layer):
    i = pl.program_id(0)
    n_used = nu_ref[0]
    half = xb_ref.shape[2]
    w_hbm = (wg_hbm, wu_hbm, wd_hbm)
    stage_refs = (sg_ref, su_ref, sd_ref)
    w_work = (wg_ref, wu_ref, wd_ref)

    def weight_copy(k, expert):
        return pltpu.make_async_copy(w_hbm[k].at[layer, expert], stage_refs[k], wsem.at[k])

    def fetch(expert):
        for k in range(3):
            weight_copy(k, expert).start()

    def land(expert):
        for k in range(3):
            weight_copy(k, expert).wait()
            w_work[k][...] = stage_refs[k][...].astype(BF16)

    @pl.when(i == 0)
    def _():
        fetch(be_ref[0])

    @pl.when(jnp.logical_and(i < n_used, first_ref[i] == 1))
    def _():
        land(be_ref[i])
        nxt = next_ref[i]

        @pl.when(nxt >= 0)
        def _():
            fetch(nxt)

    def row_copy(tok, slot, r):
        return pltpu.make_async_copy(h_hbm.at[pl.ds(tok, 1)], xb_ref.at[slot, pl.ds(r, 1)], sem.at[slot])

    def issue(blk, slot):
        base = blk * MOE_BLOCK
        for r in range(MOE_BLOCK):
            row_copy(tok_ref[base + r], slot, r).start()

    def wait_all(slot):
        for r in range(MOE_BLOCK):
            row_copy(0, slot, r).wait()

    @pl.when(i == 0)
    def _():
        issue(0, 0)

    @pl.when(i + 1 < n_used)
    def _():
        issue(i + 1, (i + 1) % 2)

    @pl.when(i < n_used)
    def _():
        slot = i % 2
        wait_all(slot)
        lo, hi = _unpack_bf16_pairs(xb_ref[slot])
        gate = _dot(lo, wg_ref[:half, :]) + _dot(hi, wg_ref[half:, :])
        up = _dot(lo, wu_ref[:half, :]) + _dot(hi, wu_ref[half:, :])
        act = (gate * _sigmoid(gate) * up).astype(BF16)
        o_ref[...] = _dot(act, wd_ref[...])

    @pl.when(i >= n_used)
    def _():
        o_ref[...] = jnp.zeros(o_ref.shape, o_ref.dtype)


def _moe_experts(hp2d, plan, layer, wg, wu, wd):
    block_e, buf_tok, n_used, first, nxt = plan
    n_blocks = block_e.shape[0]
    half = hp2d.shape[1]
    d = 2 * half
    de = wg.shape[3]
    any_spec = pl.BlockSpec(memory_space=pl.ANY)
    grid_spec = pltpu.PrefetchScalarGridSpec(
        num_scalar_prefetch=5,
        grid=(n_blocks,),
        in_specs=[any_spec, any_spec, any_spec, any_spec],
        out_specs=pl.BlockSpec((MOE_BLOCK, d), lambda i, *_: (i, 0)),
        scratch_shapes=[
            pltpu.VMEM((2, MOE_BLOCK, half), jnp.uint32), pltpu.SemaphoreType.DMA((2,)),
            pltpu.VMEM((d, de), F32), pltpu.VMEM((d, de), F32), pltpu.VMEM((de, d), F32),
            pltpu.SemaphoreType.DMA((3,)),
            pltpu.VMEM((d, de), BF16), pltpu.VMEM((d, de), BF16), pltpu.VMEM((de, d), BF16),
        ],
    )
    return pl.pallas_call(
        functools.partial(_moe_kernel, layer=layer),
        out_shape=jax.ShapeDtypeStruct((n_blocks * MOE_BLOCK, d), F32),
        grid_spec=grid_spec,
        compiler_params=_params(("arbitrary",)),
        name="moe_experts",
    )(block_e, buf_tok, n_used, first, nxt, hp2d, wg, wu, wd)


def _combine_kernel(pos_ref, ys_hbm, route_ref, x_ref, g_ref, lng_ref, lnb_ref, *rest, tm, with_h):
    if with_h:
        sc_ref, sh_ref, xo_ref, ho_ref, yb_ref, sem = rest
    else:
        xo_ref, yb_ref, sem = rest
    nt = pl.num_programs(1)
    step = pl.program_id(0) * nt + pl.program_id(1)
    total = pl.num_programs(0) * nt

    def row_copy(p, slot, k, r):
        return pltpu.make_async_copy(ys_hbm.at[pl.ds(p, 1)], yb_ref.at[slot, k, pl.ds(r, 1)], sem.at[slot])

    def issue(st, slot):
        base = TOP_K * st * tm
        for r in range(tm):
            for k in range(TOP_K):
                row_copy(pos_ref[base + TOP_K * r + k], slot, k, r).start()

    def wait_all(slot):
        for r in range(tm):
            for k in range(TOP_K):
                row_copy(0, slot, k, r).wait()

    @pl.when(step == 0)
    def _():
        issue(0, 0)

    @pl.when(step + 1 < total)
    def _():
        issue(step + 1, (step + 1) % 2)

    slot = step % 2
    wait_all(slot)
    route = route_ref[...]
    f = route[:, 2:3] * yb_ref[slot, 0] + route[:, 3:4] * yb_ref[slot, 1]
    v = ALPHA * x_ref[...] + g_ref[...] * f
    xn = _layer_norm_rows(v, lng_ref[...], lnb_ref[...])
    xo_ref[...] = xn
    if with_h:
        ho_ref[...] = (xn * (1.0 + sc_ref[...]) + sh_ref[...]).astype(ho_ref.dtype)


def _moe_combine(ys, pos, route, x, gate, lng, lnb, next_mod=None, tm=256):
    b, s, d = x.shape
    tm = min(tm, s)
    with_h = next_mod is not None
    row = lambda w: pl.BlockSpec((None, tm, w), lambda bi, i, pos: (bi, i, 0))
    per_b = lambda: pl.BlockSpec((None, 1, d), lambda bi, i, pos: (bi, 0, 0))
    const = lambda: pl.BlockSpec((1, d), lambda bi, i, pos: (0, 0))
    in_specs = [pl.BlockSpec(memory_space=pl.ANY), row(LANES), row(d), per_b(), const(), const()]
    out_specs, out_shape = [row(d)], [jax.ShapeDtypeStruct((b, s, d), F32)]
    args = [pos, ys, route, x, gate, lng, lnb]
    if with_h:
        in_specs += [per_b(), per_b()]
        out_specs.append(row(d))
        out_shape.append(jax.ShapeDtypeStruct((b, s, d), BF16))
        args += list(next_mod)
    grid_spec = pltpu.PrefetchScalarGridSpec(
        num_scalar_prefetch=1,
        grid=(b, s // tm),
        in_specs=in_specs,
        out_specs=tuple(out_specs),
        scratch_shapes=[pltpu.VMEM((2, TOP_K, tm, d), F32), pltpu.SemaphoreType.DMA((2,))],
    )
    kern = functools.partial(_combine_kernel, tm=tm, with_h=with_h)
    return pl.pallas_call(
        kern,
        out_shape=tuple(out_shape),
        grid_spec=grid_spec,
        compiler_params=_params(("arbitrary", "arbitrary")),
        name="moe_combine",
    )(*args)


def _conv_kernel(prev_ref, cur_ref, next_ref, w_ref, b_ref, g_ref, beta_ref, o_ref, buf_ref, acc_ref, sh_ref,
                 *, ts, tiles_per_seq, rs, cw):
    win_rows = sh_ref.shape[2]
    i = pl.program_id(0)
    first = (i % tiles_per_seq) == 0
    last = (i % tiles_per_seq) == tiles_per_seq - 1
    buf_ref[0:HALO, :] = jnp.where(first, 0.0, prev_ref[...])
    buf_ref[HALO:HALO + ts, :] = cur_ref[...]
    buf_ref[HALO + ts:, :] = jnp.where(last, 0.0, next_ref[...])
    d = cur_ref.shape[1]
    off = HALO - CONV_PAD

    def chunk(ci, carry):
        c0 = pl.multiple_of(ci * cw, cw)
        slot = ci % 2
        for r0 in range(0, ts, rs):
            for phase in range(1, SUBLANES):
                sh_ref[slot, phase] = buf_ref[r0 + phase:r0 + phase + win_rows, pl.ds(c0, cw)]
            acc = jnp.zeros((rs, cw), F32)
            for k in range(CONV_WIDTH):
                j = k + off
                phase, base = j % SUBLANES, j - j % SUBLANES
                if phase == 0:
                    tap = buf_ref[r0 + j:r0 + j + rs, pl.ds(c0, cw)]
                else:
                    tap = sh_ref[slot, phase, base:base + rs, :]
                acc = acc + w_ref[k:k + 1, pl.ds(c0, cw)] * tap
            acc_ref[r0:r0 + rs, pl.ds(c0, cw)] = acc
        return carry

    lax.fori_loop(0, d // cw, chunk, 0)
    u = acc_ref[...] + b_ref[...]
    z = _layer_norm_rows(u, g_ref[...], beta_ref[...])
    o_ref[...] = (z * _sigmoid(z)).astype(o_ref.dtype)


def _conv_ln_swish(u2d, seq_len, w_dw, b_dw, ln_g, ln_b, ts=256, rs=128, cw=128):
    n, d = u2d.shape
    ts = min(ts, seq_len)
    rs = min(rs, ts)
    tiles_per_seq = seq_len // ts
    hb = ts // HALO
    n_halo = n // HALO
    w_pad = jnp.zeros((CONV_WIDTH + 1, d), F32).at[:CONV_WIDTH].set(w_dw)
    last_off = HALO - CONV_PAD + CONV_WIDTH - 1
    max_base = last_off - last_off % SUBLANES
    assert max_base + SUBLANES <= 2 * HALO
    kern = functools.partial(_conv_kernel, ts=ts, tiles_per_seq=tiles_per_seq, rs=rs, cw=cw)
    const = lambda r: pl.BlockSpec((r, d), lambda i: (0, 0))
    return pl.pallas_call(
        kern,
        out_shape=jax.ShapeDtypeStruct((n, d), BF16),
        grid=(n // ts,),
        in_specs=[
            pl.BlockSpec((HALO, d), lambda i: (jnp.maximum(i * hb - 1, 0), 0)),
            pl.BlockSpec((ts, d), lambda i: (i, 0)),
            pl.BlockSpec((HALO, d), lambda i: (jnp.minimum((i + 1) * hb, n_halo - 1), 0)),
            const(CONV_WIDTH + 1), const(1), const(1), const(1),
        ],
        out_specs=pl.BlockSpec((ts, d), lambda i: (i, 0)),
        scratch_shapes=[pltpu.VMEM((ts + 2 * HALO, d), F32), pltpu.VMEM((ts, d), F32),
                        pltpu.VMEM((2, SUBLANES, rs + max_base, cw), F32)],
        compiler_params=_params(("parallel",)),
        name="conv_ln_swish",
    )(u2d, u2d, u2d, w_pad, b_dw.reshape(1, d), ln_g.reshape(1, d), ln_b.reshape(1, d))


def _rope_tables(seq_len):
    inv = ROPE_THETA ** (-jnp.arange(N_FREQ, dtype=F32) / N_FREQ)
    n_rows = seq_len // GRID_W
    row = jnp.repeat(jnp.arange(n_rows, dtype=F32), GRID_W)
    col = jnp.tile(jnp.arange(GRID_W, dtype=F32), n_rows)
    ang_r, ang_c = row[:, None] * inv, col[:, None] * inv
    cr, sr, cc, sc = jnp.cos(ang_r), jnp.sin(ang_r), jnp.cos(ang_c), jnp.sin(ang_c)
    cos = jnp.concatenate([cr, cc, cr, cc], axis=-1)
    sin = jnp.concatenate([-sr, -sc, sr, sc], axis=-1)
    return cos, sin


def _routing_plan(route2d):
    n = route2d.shape[0]
    a = n * TOP_K
    eid_f = route2d[:, :TOP_K].astype(jnp.int32).reshape(a)
    experts = jnp.arange(N_EXPERTS, dtype=jnp.int32)
    counts = jnp.sum(eid_f[:, None] == experts[None, :], axis=0, dtype=jnp.int32)
    pcounts = (counts + MOE_BLOCK - 1) // MOE_BLOCK * MOE_BLOCK
    starts = jnp.cumsum(counts) - counts
    pends = jnp.cumsum(pcounts)
    pstarts = pends - pcounts
    order = jnp.argsort(eid_f)
    rank = jnp.argsort(order)
    pos = (pstarts[eid_f] + rank - starts[eid_f]).astype(jnp.int32)
    n_blocks = -(-a // MOE_BLOCK) + N_EXPERTS
    blk_start = jnp.arange(n_blocks, dtype=jnp.int32) * MOE_BLOCK
    block_e = jnp.minimum(jnp.sum(pends[None, :] <= blk_start[:, None], axis=1, dtype=jnp.int32), N_EXPERTS - 1)
    e_slot = jnp.repeat(block_e, MOE_BLOCK)
    j = jnp.arange(n_blocks * MOE_BLOCK, dtype=jnp.int32) - pstarts[e_slot]
    src = jnp.clip(starts[e_slot] + j, 0, a - 1)
    buf_tok = jnp.where(j < counts[e_slot], order[src] // TOP_K, 0).astype(jnp.int32)
    n_used = (pends[-1:] // MOE_BLOCK).astype(jnp.int32)
    first = jnp.concatenate([jnp.ones((1,), jnp.int32), (block_e[1:] != block_e[:-1]).astype(jnp.int32)])
    later = jnp.where((experts[None, :] > experts[:, None]) & (counts[None, :] > 0), experts[None, :], N_EXPERTS)
    next_expert = jnp.min(later, axis=1)
    nxt = jnp.where(next_expert < N_EXPERTS, next_expert, -1)[block_e].astype(jnp.int32)
    return (block_e, buf_tok, n_used, first, nxt), pos


def _moe_layer(x, y, gate1, lng1, lnb1, sc2, sh2, gate2, lng2, lnb2, next_mod,
               w_rg, b_rg, w_re, b_re, layer, expert_weights):
    b, s, d = x.shape
    n = b * s
    pad = LANES - N_GROUPS - N_EXPERTS
    wr = jnp.concatenate([w_rg, w_re, jnp.zeros((d, pad), F32)], axis=1)
    br = jnp.concatenate([b_rg, b_re, jnp.zeros((pad,), F32)]).reshape(1, LANES)
    x1, hp, route = _ln_router(x, y.reshape(b, s, d), gate1, lng1, lnb1, sc2, sh2, wr, br)
    plan, pos = _routing_plan(route.reshape(n, LANES))
    ys = _moe_experts(hp.reshape(n, d // 2), plan, layer, *expert_weights)
    return _moe_combine(ys, pos, route, x1, gate2, lng2, lnb2, next_mod)


def kernel(x, c, ctx, c_ctx, w_ada, b_ada, ln_g, ln_b, w_qkv, q_gain, k_gain, w_o, w_pw1, b_pw1, w_dw, b_dw,
           conv_ln_g, conv_ln_b, w_pw2, w_rg, b_rg, w_re, b_re, w_gate, w_up, w_down):
    b, s, d = x.shape
    cl = ctx.shape[1]
    n = b * s
    n_heads = d // HEAD_DIM
    n_kv = n_heads // GQA_GROUP
    q_dim, kv_dim = n_heads * HEAD_DIM, n_kv * HEAD_DIM
    assert w_ada.shape[0] == DEPTH and b + 1 <= ADA_ROWS

    cond = jnp.zeros((ADA_ROWS, d), F32).at[:b].set(c).at[b].set(c_ctx)
    mods = _ada_modulation(cond, w_ada, b_ada)

    def mod_chunks(layer, rows):
        m = mods[layer, rows].reshape(-1, 6, d)
        return [m[:, k][:, None, :] for k in range(6)]

    row = lambda v: v.reshape(1, d)

    sh1, sc1, g1, sh2, sc2, g2 = mod_chunks(0, slice(0, b))
    csh1, csc1 = [jnp.broadcast_to(m, (b, 1, d)) for m in mod_chunks(0, slice(b, b + 1))[:2]]
    nsh1, nsc1, ng1, nsh2, nsc2, ng2 = mod_chunks(1, slice(0, b))

    h = _modulate(x, sc1, sh1)
    hc = _modulate(ctx, csc1, csh1)
    wqk = _pair_halves(w_qkv[0][:, :q_dim + kv_dim]).astype(BF16)
    wv = w_qkv[0][:, q_dim + kv_dim:].astype(BF16)
    gain = _pair_halves(jnp.concatenate([jnp.tile(q_gain[0], n_heads), jnp.tile(k_gain[0], n_kv)])).reshape(1, -1)
    scale = jnp.concatenate([jnp.full((q_dim,), ATTN_SCALE * LOG2_E, F32), jnp.ones((kv_dim,), F32)]).reshape(1, -1)
    cos, sin = _rope_tables(s)
    h2d, hc2d = h.reshape(n, d), hc.reshape(b * cl, d)
    qk = _qk_project(h2d, wqk, gain, scale, cos, sin, col_off_cols=0, n_cols=q_dim + kv_dim, tm=1024)
    v = _matmul(h2d, wv, tm=1024, out_dtype=BF16)
    one_c, zero_c = jnp.ones((cl, HEAD_DIM), F32), jnp.zeros((cl, HEAD_DIM), F32)
    kc = _qk_project(hc2d, wqk, gain, scale, one_c, zero_c, col_off_cols=q_dim, n_cols=kv_dim, tm=cl)
    vc = _matmul(hc2d, wv, tm=cl, out_dtype=BF16)
    qk = qk.reshape(b, s, -1)
    k_all = jnp.concatenate([qk[:, :, q_dim:], kc.reshape(b, cl, kv_dim)], axis=1)
    v_all = jnp.concatenate([v.reshape(b, s, n_kv, HEAD_DIM), vc.reshape(b, cl, n_kv, HEAD_DIM)], axis=1)
    sum_cols = jnp.zeros((b, s + cl, n_kv, V_EXTRA), BF16).at[..., 0].set(1.0)
    vt_all = jnp.concatenate([v_all, sum_cols], axis=-1).transpose(0, 2, 3, 1).reshape(b, n_kv * V_ROWS, s + cl)
    o = _attention(qk, k_all, vt_all, n_heads=n_heads, n_kv=n_kv)
    y = _matmul(o.reshape(n, d), w_o[0].astype(BF16))
    x, hn = _moe_layer(x, y, g1, row(ln_g[0, 0]), row(ln_b[0, 0]), sc2, sh2, g2, row(ln_g[0, 1]), row(ln_b[0, 1]),
                       (nsc1, nsh1), w_rg[0], b_rg[0], w_re[0], b_re[0],
                       0, (w_gate, w_up, w_down))

    u = _glu_matmul(hn.reshape(n, d), w_pw1[0].astype(BF16), b_pw1[0])
    cv = _conv_ln_swish(u, s, w_dw[0], b_dw[0], conv_ln_g[0], conv_ln_b[0])
    y = _matmul(cv, w_pw2[0].astype(BF16))
    (x,) = _moe_layer(x, y, ng1, row(ln_g[1, 0]), row(ln_b[1, 0]), nsc2, nsh2, ng2, row(ln_g[1, 1]), row(ln_b[1, 1]),
                      None, w_rg[1], b_rg[1], w_re[1], b_re[1], 1, (w_gate, w_up, w_down))
    return x
```

```python
import functools
import math

import jax
import jax.numpy as jnp
from jax import lax
from jax.experimental import pallas as pl
from jax.experimental.pallas import tpu as pltpu

F32 = jnp.float32
BF16 = jnp.bfloat16

HEAD_DIM = 128
GQA_GROUP = 4
GRID_W = 64
ROPE_THETA = 10000.0
N_FREQ = HEAD_DIM // 4
ATTN_SCALE = HEAD_DIM ** -0.5
LOG2_E = 1.4426950408889634
CONV_WIDTH = 31
CONV_PAD = CONV_WIDTH // 2
N_GROUPS = 4
EXPERTS_PER_GROUP = 8
N_EXPERTS = N_GROUPS * EXPERTS_PER_GROUP
TOP_K = 2
MOE_BLOCK = 256
LN_EPS = 1e-5
QK_EPS = 1e-6
DEPTH = 2
ALPHA = (2 * DEPTH) ** 0.25

V_EXTRA = 16
V_ROWS = HEAD_DIM + V_EXTRA
LANES = 128
SUBLANES = 8
HALO = 16
ADA_ROWS = 8
VMEM_LIMIT = 56 * 1024 * 1024


def _params(sem, vmem=VMEM_LIMIT, flags=None):
    return pltpu.CompilerParams(dimension_semantics=sem, vmem_limit_bytes=vmem, flags=flags)


def _sigmoid(x):
    return 1.0 / (1.0 + jnp.exp(-x))


def _dot(a, b):
    return jnp.dot(a, b, preferred_element_type=F32)


def _ada_kernel(c_ref, w_ref, b_ref, o_ref):
    c = c_ref[...]
    s = (c * _sigmoid(c)).astype(BF16)
    o_ref[...] = _dot(s, w_ref[...].astype(BF16)) + b_ref[...]


def _ada_modulation(cond, w_ada, b_ada, tn=512):
    depth, d, n6 = w_ada.shape
    return pl.pallas_call(
        _ada_kernel,
        out_shape=jax.ShapeDtypeStruct((depth, ADA_ROWS, n6), F32),
        grid=(depth, n6 // tn),
        in_specs=[
            pl.BlockSpec((ADA_ROWS, d), lambda l, j: (0, 0)),
            pl.BlockSpec((None, d, tn), lambda l, j: (l, 0, j)),
            pl.BlockSpec((None, 1, tn), lambda l, j: (l, 0, j)),
        ],
        out_specs=pl.BlockSpec((None, ADA_ROWS, tn), lambda l, j: (l, 0, j)),
        compiler_params=_params(("parallel", "parallel")),
        name="ada_modulation",
    )(cond, w_ada, b_ada.reshape(depth, 1, n6))


def _modulate_kernel(x_ref, sc_ref, sh_ref, o_ref):
    o_ref[...] = (x_ref[...] * (1.0 + sc_ref[...]) + sh_ref[...]).astype(o_ref.dtype)


def _modulate(x, sc, sh, tm=256):
    b, s, d = x.shape
    tm = min(tm, s)
    return pl.pallas_call(
        _modulate_kernel,
        out_shape=jax.ShapeDtypeStruct((b, s, d), BF16),
        grid=(b, s // tm),
        in_specs=[
            pl.BlockSpec((None, tm, d), lambda bi, i: (bi, i, 0)),
            pl.BlockSpec((None, 1, d), lambda bi, i: (bi, 0, 0)),
            pl.BlockSpec((None, 1, d), lambda bi, i: (bi, 0, 0)),
        ],
        out_specs=pl.BlockSpec((None, tm, d), lambda bi, i: (bi, i, 0)),
        compiler_params=_params(("parallel", "parallel")),
        name="modulate",
    )(x, sc, sh)


def _qk_kernel(x_ref, w_ref, gain_ref, scale_ref, cos_ref, sin_ref, o_ref, y_a, y_b):
    u = pl.program_id(0)

    @pl.when(u == 0)
    def _():
        y_b[...] = jnp.zeros(y_b.shape, F32)

    def step(y_new, y_old):
        y_new[...] = _dot(x_ref[...], w_ref[...])
        cos = cos_ref[...]
        sin = sin_ref[...]
        for h in range(o_ref.shape[1] // HEAD_DIM):
            sl = slice(h * HEAD_DIM, (h + 1) * HEAD_DIM)
            yh = y_old[:, sl]
            ms = jnp.mean(yh * yh, axis=-1, keepdims=True)
            yn = yh * lax.rsqrt(ms + QK_EPS) * gain_ref[:, sl]
            rot = yn * cos + pltpu.roll(yn, HEAD_DIM // 2, 1) * sin
            o_ref[:, sl] = (rot * scale_ref[:, sl]).astype(o_ref.dtype)

    @pl.when(u % 2 == 0)
    def _():
        step(y_a, y_b)

    @pl.when(u % 2 == 1)
    def _():
        step(y_b, y_a)


def _pair_halves(a):
    lead = a.shape[:-1]
    a = a.reshape(lead + (a.shape[-1] // HEAD_DIM, 2, 2, N_FREQ))
    return jnp.swapaxes(a, -2, -3).reshape(lead + (-1,))


def _qk_project(x2d, w, gain, scale, cos, sin, *, col_off_cols, n_cols, tm=512, tn=1024):
    m, d = x2d.shape
    tm = min(tm, m, cos.shape[0])
    tn = math.gcd(tn, n_cols, col_off_cols)
    pos_tiles = cos.shape[0] // tm
    col_off = col_off_cols // tn
    mi, nj = m // tm, n_cols // tn
    n_steps = mi * nj

    def mm_tile(u):
        t = jnp.minimum(u, n_steps - 1)
        return t % mi, t // mi

    def ep_tile(u):
        t = jnp.maximum(u - 1, 0)
        return t % mi, t // mi

    return pl.pallas_call(
        _qk_kernel,
        out_shape=jax.ShapeDtypeStruct((m, n_cols), BF16),
        grid=(n_steps + 1,),
        in_specs=[
            pl.BlockSpec((tm, d), lambda u: (mm_tile(u)[0], 0)),
            pl.BlockSpec((d, tn), lambda u: (0, mm_tile(u)[1] + col_off)),
            pl.BlockSpec((1, tn), lambda u: (0, ep_tile(u)[1] + col_off)),
            pl.BlockSpec((1, tn), lambda u: (0, ep_tile(u)[1] + col_off)),
            pl.BlockSpec((tm, HEAD_DIM), lambda u: (ep_tile(u)[0] % pos_tiles, 0)),
            pl.BlockSpec((tm, HEAD_DIM), lambda u: (ep_tile(u)[0] % pos_tiles, 0)),
        ],
        out_specs=pl.BlockSpec((tm, tn), lambda u: ep_tile(u)),
        scratch_shapes=[pltpu.VMEM((tm, tn), F32), pltpu.VMEM((tm, tn), F32)],
        compiler_params=_params(("arbitrary",)),
        name="qk_project",
    )(x2d, w, gain, scale, cos, sin)


def _attn_kernel(q_ref, k_ref, vt_ref, o_ref, qt_ref, s_bufs, p_bufs, al_bufs, m_ref, acc_ref,
                 *, tq, tk, n_chunks, rg, unroll_pairs):
    for g in range(GQA_GROUP):
        qt_ref[:, g * tq:(g + 1) * tq] = q_ref[:, g * HEAD_DIM:(g + 1) * HEAD_DIM].T
    m_ref[...] = jnp.full(m_ref.shape, -jnp.inf, F32)
    acc_ref[...] = jnp.zeros(acc_ref.shape, F32)

    def start(t):
        return pl.multiple_of(t * tk, tk)

    def scores(t, par):
        s_bufs[par] = _dot(k_ref[pl.ds(start(t), tk), :], qt_ref[...])

    def softmax(par):
        mx = s_bufs[par, 0:rg, :]
        for r in range(rg, tk, rg):
            mx = jnp.maximum(mx, s_bufs[par, r:r + rg, :])
        m_prev = m_ref[...]
        m_new = jnp.maximum(m_prev, jnp.max(mx, axis=0, keepdims=True))
        for r in range(0, tk, rg):
            p_bufs[par, r:r + rg, :] = jnp.exp2(s_bufs[par, r:r + rg, :] - m_new).astype(BF16)
        m_ref[...] = m_new
        al_bufs[par] = jnp.exp2(m_prev - m_new)

    def values(t, par):
        acc_ref[...] = al_bufs[par] * acc_ref[...] + _dot(vt_ref[:, pl.ds(start(t), tk)], p_bufs[par])

    def stage(t, par, do_scores=True, do_values=True):
        if do_scores:
            scores(t + 1, 1 - par)
        softmax(par)
        if do_values:
            values(t - 1, 1 - par)

    n = n_chunks
    scores(0, 0)
    stage(0, 0, do_scores=n > 1, do_values=False)
    n_mid = max(n - 2, 0)
    n_pairs = n_mid // 2

    def pair(i, carry):
        t = 2 * i + 1
        stage(t, 1)
        stage(t + 1, 0)
        return carry

    n_trips = n_pairs // unroll_pairs

    def trip(j, carry):
        for u in range(unroll_pairs):
            pair(j * unroll_pairs + u, carry)
        return carry

    if n_trips > 1:
        lax.fori_loop(0, n_trips, trip, 0)
    else:
        n_trips = 0
    for i in range(n_trips * unroll_pairs, n_pairs):
        pair(i, 0)
    t = 2 * n_pairs + 1
    if n_mid % 2:
        stage(t, 1)
        t += 1
    if n > 1:
        stage(t, t % 2, do_scores=False)
    values(n - 1, (n - 1) % 2)

    out = (acc_ref[:HEAD_DIM, :] / acc_ref[HEAD_DIM:HEAD_DIM + 1, :]).T
    for g in range(GQA_GROUP):
        o_ref[:, g * HEAD_DIM:(g + 1) * HEAD_DIM] = out[g * tq:(g + 1) * tq, :].astype(o_ref.dtype)


def _attention(q_src, k_all, vt_all, *, n_heads, n_kv, tq=256, tk=768):
    b, s, _ = q_src.shape
    sk = k_all.shape[1]
    tq = min(tq, s)
    assert sk % tk == 0 and s % tq == 0
    gw = GQA_GROUP * HEAD_DIM
    rows = GQA_GROUP * tq
    kern = functools.partial(_attn_kernel, tq=tq, tk=tk, n_chunks=sk // tk, rg=16, unroll_pairs=4)
    return pl.pallas_call(
        kern,
        out_shape=jax.ShapeDtypeStruct((b, s, n_heads * HEAD_DIM), BF16),
        grid=(b, n_kv, s // tq),
        in_specs=[
            pl.BlockSpec((None, tq, gw), lambda bi, h, i: (bi, i, h)),
            pl.BlockSpec((None, sk, HEAD_DIM), lambda bi, h, i: (bi, 0, h)),
            pl.BlockSpec((None, V_ROWS, sk), lambda bi, h, i: (bi, h, 0)),
        ],
        out_specs=pl.BlockSpec((None, tq, gw), lambda bi, h, i: (bi, i, h)),
        scratch_shapes=[
            pltpu.VMEM((HEAD_DIM, rows), BF16),
            pltpu.VMEM((2, tk, rows), F32),
            pltpu.VMEM((2, tk, rows), BF16),
            pltpu.VMEM((2, 1, rows), F32),
            pltpu.VMEM((1, rows), F32),
            pltpu.VMEM((V_ROWS, rows), F32),
        ],
        compiler_params=_params(("parallel", "parallel", "arbitrary")),
        name="attention",
    )(q_src, k_all, vt_all)


def _matmul_kernel(x_ref, w_ref, o_ref):
    o_ref[...] = _dot(x_ref[...], w_ref[...]).astype(o_ref.dtype)


def _matmul(x2d, w, tm=512, tn=1024, col_off_cols=0, n_cols=None, out_dtype=F32):
    m, k = x2d.shape
    n = w.shape[1] if n_cols is None else n_cols
    tm, tn = min(tm, m), math.gcd(tn, n, col_off_cols)
    col_off = col_off_cols // tn
    return pl.pallas_call(
        _matmul_kernel,
        out_shape=jax.ShapeDtypeStruct((m, n), out_dtype),
        grid=(m // tm, n // tn),
        in_specs=[pl.BlockSpec((tm, k), lambda i, j: (i, 0)), pl.BlockSpec((k, tn), lambda i, j: (0, j + col_off))],
        out_specs=pl.BlockSpec((tm, tn), lambda i, j: (i, j)),
        compiler_params=_params(("parallel", "parallel")),
        name="matmul",
    )(x2d, w)


def _glu_kernel(x_ref, wa_ref, wg_ref, ba_ref, bg_ref, o_ref):
    x = x_ref[...]
    a = _dot(x, wa_ref[...]) + ba_ref[...]
    g = _dot(x, wg_ref[...]) + bg_ref[...]
    o_ref[...] = a * _sigmoid(g)


def _glu_matmul(x2d, w, bias, tm=512, tn=512):
    m, k = x2d.shape
    n = w.shape[1] // 2
    tm, tn = min(tm, m), min(tn, n)
    nt = n // tn
    bias2 = bias.reshape(1, 2 * n)
    return pl.pallas_call(
        _glu_kernel,
        out_shape=jax.ShapeDtypeStruct((m, n), F32),
        grid=(m // tm, nt),
        in_specs=[
            pl.BlockSpec((tm, k), lambda i, j: (i, 0)),
            pl.BlockSpec((k, tn), lambda i, j: (0, j)),
            pl.BlockSpec((k, tn), lambda i, j: (0, j + nt)),
            pl.BlockSpec((1, tn), lambda i, j: (0, j)),
            pl.BlockSpec((1, tn), lambda i, j: (0, j + nt)),
        ],
        out_specs=pl.BlockSpec((tm, tn), lambda i, j: (i, j)),
        compiler_params=_params(("parallel", "parallel")),
        name="glu_matmul",
    )(x2d, w, w, bias2, bias2)


def _layer_norm_rows(v, g, b):
    mu = jnp.mean(v, axis=-1, keepdims=True)
    cen = v - mu
    var = jnp.mean(cen * cen, axis=-1, keepdims=True)
    return cen * lax.rsqrt(var + LN_EPS) * g + b


def _pack_bf16_pairs(h):
    half = h.shape[1] // 2
    hb = h.astype(BF16).astype(F32)
    lo = pltpu.bitcast(hb[:, :half], jnp.uint32)
    hi = pltpu.bitcast(hb[:, half:], jnp.uint32)
    return (lo >> 16) | (hi & jnp.uint32(0xFFFF0000))


def _unpack_bf16_pairs(w):
    lo = pltpu.bitcast(w << 16, F32).astype(BF16)
    hi = pltpu.bitcast(w & jnp.uint32(0xFFFF0000), F32).astype(BF16)
    return lo, hi


def _route(logits):
    lane = lax.broadcasted_iota(jnp.int32, logits.shape, 1).astype(F32)
    neg = -jnp.inf
    big = float(LANES)
    is_g = lane < N_GROUPS
    gl = jnp.where(is_g, logits, neg)
    gmax = jnp.max(gl, axis=-1, keepdims=True)
    gidx = jnp.min(jnp.where(gl == gmax, lane, big), axis=-1, keepdims=True)
    gsum = jnp.sum(jnp.where(is_g, jnp.exp(gl - gmax), 0.0), axis=-1, keepdims=True)
    gw = 1.0 / gsum
    lo = N_GROUPS + EXPERTS_PER_GROUP * gidx
    el = jnp.where(lane >= lo, jnp.where(lane < lo + EXPERTS_PER_GROUP, logits, neg), neg)
    t1 = jnp.max(el, axis=-1, keepdims=True)
    i1 = jnp.min(jnp.where(el == t1, lane, big), axis=-1, keepdims=True)
    el2 = jnp.where(lane == i1, neg, el)
    t2 = jnp.max(el2, axis=-1, keepdims=True)
    i2 = jnp.min(jnp.where(el2 == t2, lane, big), axis=-1, keepdims=True)
    e = jnp.exp(t2 - t1)
    w1 = gw / (1.0 + e)
    w2 = gw * e / (1.0 + e)
    out = jnp.where(lane == 0.0, i1 - N_GROUPS, 0.0)
    out = jnp.where(lane == 1.0, i2 - N_GROUPS, out)
    out = jnp.where(lane == 2.0, w1, out)
    out = jnp.where(lane == 3.0, w2, out)
    return out


def _ln_router_kernel(x_ref, y_ref, g_ref, lng_ref, lnb_ref, sc_ref, sh_ref, wr_ref, br_ref,
                      xo_ref, hp_ref, route_ref):
    v = ALPHA * x_ref[...] + g_ref[...] * y_ref[...]
    xn = _layer_norm_rows(v, lng_ref[...], lnb_ref[...])
    xo_ref[...] = xn
    h = xn * (1.0 + sc_ref[...]) + sh_ref[...]
    hp_ref[...] = _pack_bf16_pairs(h)
    h_hi = h.astype(BF16)
    h_lo = (h - h_hi.astype(F32)).astype(BF16)
    wr = wr_ref[...]
    w_hi = wr.astype(BF16)
    w_lo = (wr - w_hi.astype(F32)).astype(BF16)
    logits = _dot(h_hi, w_hi) + _dot(h_lo, w_hi) + _dot(h_hi, w_lo) + br_ref[...]
    route_ref[...] = _route(logits)


def _ln_router(x, y, gate, lng, lnb, sc, sh, wr, br, tm=256):
    b, s, d = x.shape
    tm = min(tm, s)
    row = lambda: pl.BlockSpec((None, tm, d), lambda bi, i: (bi, i, 0))
    per_b = lambda: pl.BlockSpec((None, 1, d), lambda bi, i: (bi, 0, 0))
    const = lambda shape: pl.BlockSpec(shape, lambda bi, i: (0,) * len(shape))
    return pl.pallas_call(
        _ln_router_kernel,
        out_shape=(
            jax.ShapeDtypeStruct((b, s, d), F32),
            jax.ShapeDtypeStruct((b, s, d // 2), jnp.uint32),
            jax.ShapeDtypeStruct((b, s, LANES), F32),
        ),
        grid=(b, s // tm),
        in_specs=[row(), row(), per_b(), const((1, d)), const((1, d)), per_b(), per_b(),
                  const((d, LANES)), const((1, LANES))],
        out_specs=(
            row(),
            pl.BlockSpec((None, tm, d // 2), lambda bi, i: (bi, i, 0)),
            pl.BlockSpec((None, tm, LANES), lambda bi, i: (bi, i, 0)),
        ),
        compiler_params=_params(("parallel", "parallel")),
        name="ln_router",
    )(x, y, gate, lng, lnb, sc, sh, wr, br)


def _moe_kernel(be_ref, tok_ref, nu_ref, first_ref, next_ref, h_hbm, wg_hbm, wu_hbm, wd_hbm, o_ref,
                xb_ref, sem, sg_ref, su_ref, sd_ref, wsem, wg_ref, wu_ref, wd_ref, *, layer):
    i = pl.program_id(0)
    n_used = nu_ref[0]
    half = xb_ref.shape[2]
    w_hbm = (wg_hbm, wu_hbm, wd_hbm)
    stage_refs = (sg_ref, su_ref, sd_ref)
    w_work = (wg_ref, wu_ref, wd_ref)

    def weight_copy(k, expert):
        return pltpu.make_async_copy(w_hbm[k].at[layer, expert], stage_refs[k], wsem.at[k])

    def fetch(expert):
        for k in range(3):
            weight_copy(k, expert).start(priority=1)

    def land(expert):
        for k in range(3):
            weight_copy(k, expert).wait()
            w_work[k][...] = stage_refs[k][...].astype(BF16)

    @pl.when(i == 0)
    def _():
        fetch(be_ref[0])

    @pl.when(jnp.logical_and(i < n_used, first_ref[i] == 1))
    def _():
        land(be_ref[i])
        nxt = next_ref[i]

        @pl.when(nxt >= 0)
        def _():
            fetch(nxt)

    def row_copy(tok, slot, r):
        return pltpu.make_async_copy(h_hbm.at[pl.ds(tok, 1)], xb_ref.at[slot, pl.ds(r, 1)], sem.at[slot])

    def issue(blk, slot):
        base = blk * MOE_BLOCK
        for r in range(MOE_BLOCK):
            row_copy(tok_ref[base + r], slot, r).start()

    def wait_all(slot):
        for r in range(MOE_BLOCK):
            row_copy(0, slot, r).wait()

    @pl.when(i == 0)
    def _():
        issue(0, 0)

    @pl.when(i + 1 < n_used)
    def _():
        issue(i + 1, (i + 1) % 2)

    @pl.when(i < n_used)
    def _():
        slot = i % 2
        wait_all(slot)
        lo, hi = _unpack_bf16_pairs(xb_ref[slot])
        gate = _dot(lo, wg_ref[:half, :]) + _dot(hi, wg_ref[half:, :])
        up = _dot(lo, wu_ref[:half, :]) + _dot(hi, wu_ref[half:, :])
        act = (gate * _sigmoid(gate) * up).astype(BF16)
        o_ref[...] = _dot(act, wd_ref[...])

    @pl.when(i >= n_used)
    def _():
        o_ref[...] = jnp.zeros(o_ref.shape, o_ref.dtype)


def _moe_experts(hp2d, plan, layer, wg, wu, wd):
    block_e, buf_tok, n_used, first, nxt = plan
    n_blocks = block_e.shape[0]
    half = hp2d.shape[1]
    d = 2 * half
    de = wg.shape[3]
    any_spec = pl.BlockSpec(memory_space=pl.ANY)
    grid_spec = pltpu.PrefetchScalarGridSpec(
        num_scalar_prefetch=5,
        grid=(n_blocks,),
        in_specs=[any_spec, any_spec, any_spec, any_spec],
        out_specs=pl.BlockSpec((MOE_BLOCK, d), lambda i, *_: (i, 0)),
        scratch_shapes=[
            pltpu.VMEM((2, MOE_BLOCK, half), jnp.uint32), pltpu.SemaphoreType.DMA((2,)),
            pltpu.VMEM((d, de), F32), pltpu.VMEM((d, de), F32), pltpu.VMEM((de, d), F32),
            pltpu.SemaphoreType.DMA((3,)),
            pltpu.VMEM((d, de), BF16), pltpu.VMEM((d, de), BF16), pltpu.VMEM((de, d), BF16),
        ],
    )
    return pl.pallas_call(
        functools.partial(_moe_kernel, layer=layer),
        out_shape=jax.ShapeDtypeStruct((n_blocks * MOE_BLOCK, d), F32),
        grid_spec=grid_spec,
        compiler_params=_params(("arbitrary",)),
        name="moe_experts",
    )(block_e, buf_tok, n_used, first, nxt, hp2d, wg, wu, wd)


def _combine_kernel(pos_ref, ys_hbm, route_ref, x_ref, g_ref, lng_ref, lnb_ref, *rest, tm, with_h):
    if with_h:
        sc_ref, sh_ref, xo_ref, ho_ref, yb_ref, sem = rest
    else:
        xo_ref, yb_ref, sem = rest
    nt = pl.num_programs(1)
    step = pl.program_id(0) * nt + pl.program_id(1)
    total = pl.num_programs(0) * nt

    def row_copy(p, slot, k, r):
        return pltpu.make_async_copy(ys_hbm.at[pl.ds(p, 1)], yb_ref.at[slot, k, pl.ds(r, 1)], sem.at[slot])

    def issue(st, slot):
        base = TOP_K * st * tm
        for r in range(tm):
            for k in range(TOP_K):
                row_copy(pos_ref[base + TOP_K * r + k], slot, k, r).start()

    def wait_all(slot):
        for r in range(tm):
            for k in range(TOP_K):
                row_copy(0, slot, k, r).wait()

    @pl.when(step == 0)
    def _():
        issue(0, 0)

    @pl.when(step + 1 < total)
    def _():
        issue(step + 1, (step + 1) % 2)

    slot = step % 2
    wait_all(slot)
    route = route_ref[...]
    f = route[:, 2:3] * yb_ref[slot, 0] + route[:, 3:4] * yb_ref[slot, 1]
    v = ALPHA * x_ref[...] + g_ref[...] * f
    xn = _layer_norm_rows(v, lng_ref[...], lnb_ref[...])
    xo_ref[...] = xn
    if with_h:
        ho_ref[...] = (xn * (1.0 + sc_ref[...]) + sh_ref[...]).astype(ho_ref.dtype)


def _moe_combine(ys, pos, route, x, gate, lng, lnb, next_mod=None, tm=256):
    b, s, d = x.shape
    tm = min(tm, s)
    with_h = next_mod is not None
    row = lambda w: pl.BlockSpec((None, tm, w), lambda bi, i, pos: (bi, i, 0))
    per_b = lambda: pl.BlockSpec((None, 1, d), lambda bi, i, pos: (bi, 0, 0))
    const = lambda: pl.BlockSpec((1, d), lambda bi, i, pos: (0, 0))
    in_specs = [pl.BlockSpec(memory_space=pl.ANY), row(LANES), row(d), per_b(), const(), const()]
    out_specs, out_shape = [row(d)], [jax.ShapeDtypeStruct((b, s, d), F32)]
    args = [pos, ys, route, x, gate, lng, lnb]
    if with_h:
        in_specs += [per_b(), per_b()]
        out_specs.append(row(d))
        out_shape.append(jax.ShapeDtypeStruct((b, s, d), BF16))
        args += list(next_mod)
    grid_spec = pltpu.PrefetchScalarGridSpec(
        num_scalar_prefetch=1,
        grid=(b, s // tm),
        in_specs=in_specs,
        out_specs=tuple(out_specs),
        scratch_shapes=[pltpu.VMEM((2, TOP_K, tm, d), F32), pltpu.SemaphoreType.DMA((2,))],
    )
    kern = functools.partial(_combine_kernel, tm=tm, with_h=with_h)
    return pl.pallas_call(
        kern,
        out_shape=tuple(out_shape),
        grid_spec=grid_spec,
        compiler_params=_params(("arbitrary", "arbitrary")),
        name="moe_combine",
    )(*args)


def _conv_kernel(prev_ref, cur_ref, next_ref, w_ref, b_ref, g_ref, beta_ref, o_ref, buf_ref, acc_ref, sh_ref,
                 *, ts, tiles_per_seq, rs, cw):
    win_rows = sh_ref.shape[2]
    i = pl.program_id(0)
    first = (i % tiles_per_seq) == 0
    last = (i % tiles_per_seq) == tiles_per_seq - 1
    buf_ref[0:HALO, :] = jnp.where(first, 0.0, prev_ref[...])
    buf_ref[HALO:HALO + ts, :] = cur_ref[...]
    buf_ref[HALO + ts:, :] = jnp.where(last, 0.0, next_ref[...])
    d = cur_ref.shape[1]
    off = HALO - CONV_PAD

    def chunk(ci, carry):
        c0 = pl.multiple_of(ci * cw, cw)
        slot = ci % 2
        for r0 in range(0, ts, rs):
            for phase in range(1, SUBLANES):
                sh_ref[slot, phase] = buf_ref[r0 + phase:r0 + phase + win_rows, pl.ds(c0, cw)]
            acc = jnp.zeros((rs, cw), F32)
            for k in range(CONV_WIDTH):
                j = k + off
                phase, base = j % SUBLANES, j - j % SUBLANES
                if phase == 0:
                    tap = buf_ref[r0 + j:r0 + j + rs, pl.ds(c0, cw)]
                else:
                    tap = sh_ref[slot, phase, base:base + rs, :]
                acc = acc + w_ref[k:k + 1, pl.ds(c0, cw)] * tap
            acc_ref[r0:r0 + rs, pl.ds(c0, cw)] = acc
        return carry

    lax.fori_loop(0, d // cw, chunk, 0)
    u = acc_ref[...] + b_ref[...]
    z = _layer_norm_rows(u, g_ref[...], beta_ref[...])
    o_ref[...] = (z * _sigmoid(z)).astype(o_ref.dtype)


def _conv_ln_swish(u2d, seq_len, w_dw, b_dw, ln_g, ln_b, ts=256, rs=128, cw=128):
    n, d = u2d.shape
    ts = min(ts, seq_len)
    rs = min(rs, ts)
    tiles_per_seq = seq_len // ts
    hb = ts // HALO
    n_halo = n // HALO
    w_pad = jnp.zeros((CONV_WIDTH + 1, d), F32).at[:CONV_WIDTH].set(w_dw)
    last_off = HALO - CONV_PAD + CONV_WIDTH - 1
    max_base = last_off - last_off % SUBLANES
    assert max_base + SUBLANES <= 2 * HALO
    kern = functools.partial(_conv_kernel, ts=ts, tiles_per_seq=tiles_per_seq, rs=rs, cw=cw)
    const = lambda r: pl.BlockSpec((r, d), lambda i: (0, 0))
    return pl.pallas_call(
        kern,
        out_shape=jax.ShapeDtypeStruct((n, d), BF16),
        grid=(n // ts,),
        in_specs=[
            pl.BlockSpec((HALO, d), lambda i: (jnp.maximum(i * hb - 1, 0), 0)),
            pl.BlockSpec((ts, d), lambda i: (i, 0)),
            pl.BlockSpec((HALO, d), lambda i: (jnp.minimum((i + 1) * hb, n_halo - 1), 0)),
            const(CONV_WIDTH + 1), const(1), const(1), const(1),
        ],
        out_specs=pl.BlockSpec((ts, d), lambda i: (i, 0)),
        scratch_shapes=[pltpu.VMEM((ts + 2 * HALO, d), F32), pltpu.VMEM((ts, d), F32),
                        pltpu.VMEM((2, SUBLANES, rs + max_base, cw), F32)],
        compiler_params=_params(("parallel",)),
        name="conv_ln_swish",
    )(u2d, u2d, u2d, w_pad, b_dw.reshape(1, d), ln_g.reshape(1, d), ln_b.reshape(1, d))


def _rope_tables(seq_len):
    inv = ROPE_THETA ** (-jnp.arange(N_FREQ, dtype=F32) / N_FREQ)
    n_rows = seq_len // GRID_W
    row = jnp.repeat(jnp.arange(n_rows, dtype=F32), GRID_W)
    col = jnp.tile(jnp.arange(GRID_W, dtype=F32), n_rows)
    ang_r, ang_c = row[:, None] * inv, col[:, None] * inv
    cr, sr, cc, sc = jnp.cos(ang_r), jnp.sin(ang_r), jnp.cos(ang_c), jnp.sin(ang_c)
    cos = jnp.concatenate([cr, cc, cr, cc], axis=-1)
    sin = jnp.concatenate([-sr, -sc, sr, sc], axis=-1)
    return cos, sin


def _routing_plan(route2d):
    n = route2d.shape[0]
    a = n * TOP_K
    eid_f = route2d[:, :TOP_K].astype(jnp.int32).reshape(a)
    experts = jnp.arange(N_EXPERTS, dtype=jnp.int32)
    counts = jnp.sum(eid_f[:, None] == experts[None, :], axis=0, dtype=jnp.int32)
    pcounts = (counts + MOE_BLOCK - 1) // MOE_BLOCK * MOE_BLOCK
    starts = jnp.cumsum(counts) - counts
    pends = jnp.cumsum(pcounts)
    pstarts = pends - pcounts
    order = jnp.argsort(eid_f)
    rank = jnp.argsort(order)
    pos = (pstarts[eid_f] + rank - starts[eid_f]).astype(jnp.int32)
    n_blocks = -(-a // MOE_BLOCK) + N_EXPERTS
    blk_start = jnp.arange(n_blocks, dtype=jnp.int32) * MOE_BLOCK
    block_e = jnp.minimum(jnp.sum(pends[None, :] <= blk_start[:, None], axis=1, dtype=jnp.int32), N_EXPERTS - 1)
    e_slot = jnp.repeat(block_e, MOE_BLOCK)
    j = jnp.arange(n_blocks * MOE_BLOCK, dtype=jnp.int32) - pstarts[e_slot]
    src = jnp.clip(starts[e_slot] + j, 0, a - 1)
    buf_tok = jnp.where(j < counts[e_slot], order[src] // TOP_K, 0).astype(jnp.int32)
    n_used = (pends[-1:] // MOE_BLOCK).astype(jnp.int32)
    first = jnp.concatenate([jnp.ones((1,), jnp.int32), (block_e[1:] != block_e[:-1]).astype(jnp.int32)])
    later = jnp.where((experts[None, :] > experts[:, None]) & (counts[None, :] > 0), experts[None, :], N_EXPERTS)
    next_expert = jnp.min(later, axis=1)
    nxt = jnp.where(next_expert < N_EXPERTS, next_expert, -1)[block_e].astype(jnp.int32)
    return (block_e, buf_tok, n_used, first, nxt), pos


def _moe_layer(x, y, gate1, lng1, lnb1, sc2, sh2, gate2, lng2, lnb2, next_mod,
               w_rg, b_rg, w_re, b_re, layer, expert_weights):
    b, s, d = x.shape
    n = b * s
    pad = LANES - N_GROUPS - N_EXPERTS
    wr = jnp.concatenate([w_rg, w_re, jnp.zeros((d, pad), F32)], axis=1)
    br = jnp.concatenate([b_rg, b_re, jnp.zeros((pad,), F32)]).reshape(1, LANES)
    x1, hp, route = _ln_router(x, y.reshape(b, s, d), gate1, lng1, lnb1, sc2, sh2, wr, br)
    plan, pos = _routing_plan(route.reshape(n, LANES))
    ys = _moe_experts(hp.reshape(n, d // 2), plan, layer, *expert_weights)
    return _moe_combine(ys, pos, route, x1, gate2, lng2, lnb2, next_mod)


def kernel(x, c, ctx, c_ctx, w_ada, b_ada, ln_g, ln_b, w_qkv, q_gain, k_gain, w_o, w_pw1, b_pw1, w_dw, b_dw,
           conv_ln_g, conv_ln_b, w_pw2, w_rg, b_rg, w_re, b_re, w_gate, w_up, w_down):
    b, s, d = x.shape
    cl = ctx.shape[1]
    n = b * s
    n_heads = d // HEAD_DIM
    n_kv = n_heads // GQA_GROUP
    q_dim, kv_dim = n_heads * HEAD_DIM, n_kv * HEAD_DIM
    assert w_ada.shape[0] == DEPTH and b + 1 <= ADA_ROWS

    cond = jnp.zeros((ADA_ROWS, d), F32).at[:b].set(c).at[b].set(c_ctx)
    mods = _ada_modulation(cond, w_ada, b_ada)

    def mod_chunks(layer, rows):
        m = mods[layer, rows].reshape(-1, 6, d)
        return [m[:, k][:, None, :] for k in range(6)]

    row = lambda v: v.reshape(1, d)

    sh1, sc1, g1, sh2, sc2, g2 = mod_chunks(0, slice(0, b))
    csh1, csc1 = [jnp.broadcast_to(m, (b, 1, d)) for m in mod_chunks(0, slice(b, b + 1))[:2]]
    nsh1, nsc1, ng1, nsh2, nsc2, ng2 = mod_chunks(1, slice(0, b))

    h = _modulate(x, sc1, sh1)
    hc = _modulate(ctx, csc1, csh1)
    wqk = _pair_halves(w_qkv[0][:, :q_dim + kv_dim]).astype(BF16)
    wv = w_qkv[0][:, q_dim + kv_dim:].astype(BF16)
    gain = _pair_halves(jnp.concatenate([jnp.tile(q_gain[0], n_heads), jnp.tile(k_gain[0], n_kv)])).reshape(1, -1)
    scale = jnp.concatenate([jnp.full((q_dim,), ATTN_SCALE * LOG2_E, F32), jnp.ones((kv_dim,), F32)]).reshape(1, -1)
    cos, sin = _rope_tables(s)
    h2d, hc2d = h.reshape(n, d), hc.reshape(b * cl, d)
    qk = _qk_project(h2d, wqk, gain, scale, cos, sin, col_off_cols=0, n_cols=q_dim + kv_dim)
    v = _matmul(h2d, wv, tm=1024, out_dtype=BF16)
    one_c, zero_c = jnp.ones((cl, HEAD_DIM), F32), jnp.zeros((cl, HEAD_DIM), F32)
    kc = _qk_project(hc2d, wqk, gain, scale, one_c, zero_c, col_off_cols=q_dim, n_cols=kv_dim, tm=cl)
    vc = _matmul(hc2d, wv, tm=cl, out_dtype=BF16)
    qk = qk.reshape(b, s, -1)
    k_all = jnp.concatenate([qk[:, :, q_dim:], kc.reshape(b, cl, kv_dim)], axis=1)
    v_all = jnp.concatenate([v.reshape(b, s, n_kv, HEAD_DIM), vc.reshape(b, cl, n_kv, HEAD_DIM)], axis=1)
    sum_cols = jnp.zeros((b, s + cl, n_kv, V_EXTRA), BF16).at[..., 0].set(1.0)
    vt_all = jnp.concatenate([v_all, sum_cols], axis=-1).transpose(0, 2, 3, 1).reshape(b, n_kv * V_ROWS, s + cl)
    o = _attention(qk, k_all, vt_all, n_heads=n_heads, n_kv=n_kv)
    y = _matmul(o.reshape(n, d), w_o[0].astype(BF16))
    x, hn = _moe_layer(x, y, g1, row(ln_g[0, 0]), row(ln_b[0, 0]), sc2, sh2, g2, row(ln_g[0, 1]), row(ln_b[0, 1]),
                       (nsc1, nsh1), w_rg[0], b_rg[0], w_re[0], b_re[0],
                       0, (w_gate, w_up, w_down))

    u = _glu_matmul(hn.reshape(n, d), w_pw1[0].astype(BF16), b_pw1[0])
    cv = _conv_ln_swish(u, s, w_dw[0], b_dw[0], conv_ln_g[0], conv_ln_b[0])
    y = _matmul(cv, w_pw2[0].astype(BF16))
    (x,) = _moe_layer(x, y, ng1, row(ln_g[1, 0]), row(ln_b[1, 0]), nsc2, nsh2, ng2, row(ln_g[1, 1]), row(ln_b[1, 1]),
                      None, w_rg[1], b_rg[1], w_re[1], b_re[1], 1, (w_gate, w_up, w_down))
    return x
```

```python
import functools
import math

import jax
import jax.numpy as jnp
from jax import lax
from jax.experimental import pallas as pl
from jax.experimental.pallas import tpu as pltpu

F32 = jnp.float32
BF16 = jnp.bfloat16

HEAD_DIM = 128
GQA_GROUP = 4
GRID_W = 64
ROPE_THETA = 10000.0
N_FREQ = HEAD_DIM // 4
ATTN_SCALE = HEAD_DIM ** -0.5
LOG2_E = 1.4426950408889634
CONV_WIDTH = 31
CONV_PAD = CONV_WIDTH // 2
N_GROUPS = 4
EXPERTS_PER_GROUP = 8
N_EXPERTS = N_GROUPS * EXPERTS_PER_GROUP
TOP_K = 2
MOE_BLOCK = 256
LN_EPS = 1e-5
QK_EPS = 1e-6
DEPTH = 2
ALPHA = (2 * DEPTH) ** 0.25

ATTN_BUFFERS = 3
V_EXTRA = 16
V_ROWS = HEAD_DIM + V_EXTRA
LANES = 128
SUBLANES = 8
HALO = 16
ADA_ROWS = 8
VMEM_LIMIT = 56 * 1024 * 1024


def _params(sem, vmem=VMEM_LIMIT, flags=None):
    return pltpu.CompilerParams(dimension_semantics=sem, vmem_limit_bytes=vmem, flags=flags)


def _sigmoid(x):
    return 1.0 / (1.0 + jnp.exp(-x))


def _dot(a, b):
    return jnp.dot(a, b, preferred_element_type=F32)


def _ada_kernel(c_ref, w_ref, b_ref, o_ref):
    c = c_ref[...]
    s = (c * _sigmoid(c)).astype(BF16)
    o_ref[...] = _dot(s, w_ref[...].astype(BF16)) + b_ref[...]


def _ada_modulation(cond, w_ada, b_ada, tn=512):
    depth, d, n6 = w_ada.shape
    return pl.pallas_call(
        _ada_kernel,
        out_shape=jax.ShapeDtypeStruct((depth, ADA_ROWS, n6), F32),
        grid=(depth, n6 // tn),
        in_specs=[
            pl.BlockSpec((ADA_ROWS, d), lambda l, j: (0, 0)),
            pl.BlockSpec((None, d, tn), lambda l, j: (l, 0, j)),
            pl.BlockSpec((None, 1, tn), lambda l, j: (l, 0, j)),
        ],
        out_specs=pl.BlockSpec((None, ADA_ROWS, tn), lambda l, j: (l, 0, j)),
        compiler_params=_params(("parallel", "parallel")),
        name="ada_modulation",
    )(cond, w_ada, b_ada.reshape(depth, 1, n6))


def _modulate_kernel(x_ref, sc_ref, sh_ref, o_ref):
    o_ref[...] = (x_ref[...] * (1.0 + sc_ref[...]) + sh_ref[...]).astype(o_ref.dtype)


def _modulate(x, sc, sh, tm=256):
    b, s, d = x.shape
    tm = min(tm, s)
    return pl.pallas_call(
        _modulate_kernel,
        out_shape=jax.ShapeDtypeStruct((b, s, d), BF16),
        grid=(b, s // tm),
        in_specs=[
            pl.BlockSpec((None, tm, d), lambda bi, i: (bi, i, 0)),
            pl.BlockSpec((None, 1, d), lambda bi, i: (bi, 0, 0)),
            pl.BlockSpec((None, 1, d), lambda bi, i: (bi, 0, 0)),
        ],
        out_specs=pl.BlockSpec((None, tm, d), lambda bi, i: (bi, i, 0)),
        compiler_params=_params(("parallel", "parallel")),
        name="modulate",
    )(x, sc, sh)


def _qk_kernel(x_ref, w_ref, gain_ref, scale_ref, cos_ref, sin_ref, o_ref, y_a, y_b):
    u = pl.program_id(0)

    @pl.when(u == 0)
    def _():
        y_b[...] = jnp.zeros(y_b.shape, F32)

    def step(y_new, y_old):
        y_new[...] = _dot(x_ref[...], w_ref[...])
        cos = cos_ref[...]
        sin = sin_ref[...]
        for h in range(o_ref.shape[1] // HEAD_DIM):
            sl = slice(h * HEAD_DIM, (h + 1) * HEAD_DIM)
            yh = y_old[:, sl]
            ms = jnp.mean(yh * yh, axis=-1, keepdims=True)
            yn = yh * lax.rsqrt(ms + QK_EPS) * gain_ref[:, sl]
            rot = yn * cos + pltpu.roll(yn, HEAD_DIM // 2, 1) * sin
            o_ref[:, sl] = (rot * scale_ref[:, sl]).astype(o_ref.dtype)

    @pl.when(u % 2 == 0)
    def _():
        step(y_a, y_b)

    @pl.when(u % 2 == 1)
    def _():
        step(y_b, y_a)


def _pair_halves(a):
    lead = a.shape[:-1]
    a = a.reshape(lead + (a.shape[-1] // HEAD_DIM, 2, 2, N_FREQ))
    return jnp.swapaxes(a, -2, -3).reshape(lead + (-1,))


def _qk_project(x2d, w, gain, scale, cos, sin, *, col_off_cols, n_cols, tm=512, tn=1024):
    m, d = x2d.shape
    tm = min(tm, m, cos.shape[0])
    tn = math.gcd(tn, n_cols, col_off_cols)
    pos_tiles = cos.shape[0] // tm
    col_off = col_off_cols // tn
    mi, nj = m // tm, n_cols // tn
    n_steps = mi * nj

    def mm_tile(u):
        t = jnp.minimum(u, n_steps - 1)
        return t % mi, t // mi

    def ep_tile(u):
        t = jnp.maximum(u - 1, 0)
        return t % mi, t // mi

    return pl.pallas_call(
        _qk_kernel,
        out_shape=jax.ShapeDtypeStruct((m, n_cols), BF16),
        grid=(n_steps + 1,),
        in_specs=[
            pl.BlockSpec((tm, d), lambda u: (mm_tile(u)[0], 0)),
            pl.BlockSpec((d, tn), lambda u: (0, mm_tile(u)[1] + col_off)),
            pl.BlockSpec((1, tn), lambda u: (0, ep_tile(u)[1] + col_off)),
            pl.BlockSpec((1, tn), lambda u: (0, ep_tile(u)[1] + col_off)),
            pl.BlockSpec((tm, HEAD_DIM), lambda u: (ep_tile(u)[0] % pos_tiles, 0)),
            pl.BlockSpec((tm, HEAD_DIM), lambda u: (ep_tile(u)[0] % pos_tiles, 0)),
        ],
        out_specs=pl.BlockSpec((tm, tn), lambda u: ep_tile(u)),
        scratch_shapes=[pltpu.VMEM((tm, tn), F32), pltpu.VMEM((tm, tn), F32)],
        compiler_params=_params(("arbitrary",)),
        name="qk_project",
    )(x2d, w, gain, scale, cos, sin)


def _attn_kernel(q_ref, k_ref, vt_ref, o_ref, qt_ref, s_bufs, p_bufs, al_bufs, m_ref, acc_ref,
                 *, tq, tk, n_chunks, rg):
    for g in range(GQA_GROUP):
        qt_ref[:, g * tq:(g + 1) * tq] = q_ref[:, g * HEAD_DIM:(g + 1) * HEAD_DIM].T
    m_ref[...] = jnp.full(m_ref.shape, -jnp.inf, F32)
    acc_ref[...] = jnp.zeros(acc_ref.shape, F32)

    def start(t):
        return t * tk

    def scores(t, par):
        s_bufs[par] = _dot(k_ref[pl.ds(start(t), tk), :], qt_ref[...])

    def softmax(par):
        mx = s_bufs[par, 0:rg, :]
        for r in range(rg, tk, rg):
            mx = jnp.maximum(mx, s_bufs[par, r:r + rg, :])
        m_prev = m_ref[...]
        m_new = jnp.maximum(m_prev, jnp.max(mx, axis=0, keepdims=True))
        for r in range(0, tk, rg):
            p_bufs[par, r:r + rg, :] = jnp.exp2(s_bufs[par, r:r + rg, :] - m_new).astype(BF16)
        m_ref[...] = m_new
        al_bufs[par] = jnp.exp2(m_prev - m_new)

    def values(t, par):
        acc_ref[...] = al_bufs[par] * acc_ref[...] + _dot(vt_ref[:, pl.ds(start(t), tk)], p_bufs[par])

    n_buf = s_bufs.shape[0]
    scores(0, 0)
    for t in range(n_chunks):
        if t + 1 < n_chunks:
            scores(t + 1, (t + 1) % n_buf)
        softmax(t % n_buf)
        if t >= 1:
            values(t - 1, (t - 1) % n_buf)
    values(n_chunks - 1, (n_chunks - 1) % n_buf)

    out = (acc_ref[:HEAD_DIM, :] / acc_ref[HEAD_DIM:HEAD_DIM + 1, :]).T
    for g in range(GQA_GROUP):
        o_ref[:, g * HEAD_DIM:(g + 1) * HEAD_DIM] = out[g * tq:(g + 1) * tq, :].astype(o_ref.dtype)


def _attention(q_src, k_all, vt_all, *, n_heads, n_kv, tq=256, tk=768):
    b, s, _ = q_src.shape
    sk = k_all.shape[1]
    tq = min(tq, s)
    assert sk % tk == 0 and s % tq == 0
    gw = GQA_GROUP * HEAD_DIM
    rows = GQA_GROUP * tq
    kern = functools.partial(_attn_kernel, tq=tq, tk=tk, n_chunks=sk // tk, rg=16)
    return pl.pallas_call(
        kern,
        out_shape=jax.ShapeDtypeStruct((b, s, n_heads * HEAD_DIM), BF16),
        grid=(b, n_kv, s // tq),
        in_specs=[
            pl.BlockSpec((None, tq, gw), lambda bi, h, i: (bi, i, h)),
            pl.BlockSpec((None, sk, HEAD_DIM), lambda bi, h, i: (bi, 0, h)),
            pl.BlockSpec((None, V_ROWS, sk), lambda bi, h, i: (bi, h, 0)),
        ],
        out_specs=pl.BlockSpec((None, tq, gw), lambda bi, h, i: (bi, i, h)),
        scratch_shapes=[
            pltpu.VMEM((HEAD_DIM, rows), BF16),
            pltpu.VMEM((ATTN_BUFFERS, tk, rows), F32),
            pltpu.VMEM((ATTN_BUFFERS, tk, rows), BF16),
            pltpu.VMEM((ATTN_BUFFERS, 1, rows), F32),
            pltpu.VMEM((1, rows), F32),
            pltpu.VMEM((V_ROWS, rows), F32),
        ],
        compiler_params=_params(("parallel", "parallel", "arbitrary")),
        name="attention",
    )(q_src, k_all, vt_all)


def _matmul_kernel(x_ref, w_ref, o_ref):
    o_ref[...] = _dot(x_ref[...], w_ref[...]).astype(o_ref.dtype)


def _matmul(x2d, w, tm=512, tn=1024, col_off_cols=0, n_cols=None, out_dtype=F32):
    m, k = x2d.shape
    n = w.shape[1] if n_cols is None else n_cols
    tm, tn = min(tm, m), math.gcd(tn, n, col_off_cols)
    col_off = col_off_cols // tn
    return pl.pallas_call(
        _matmul_kernel,
        out_shape=jax.ShapeDtypeStruct((m, n), out_dtype),
        grid=(m // tm, n // tn),
        in_specs=[pl.BlockSpec((tm, k), lambda i, j: (i, 0)), pl.BlockSpec((k, tn), lambda i, j: (0, j + col_off))],
        out_specs=pl.BlockSpec((tm, tn), lambda i, j: (i, j)),
        compiler_params=_params(("parallel", "parallel")),
        name="matmul",
    )(x2d, w)


def _glu_kernel(x_ref, wa_ref, wg_ref, ba_ref, bg_ref, o_ref):
    x = x_ref[...]
    a = _dot(x, wa_ref[...]) + ba_ref[...]
    g = _dot(x, wg_ref[...]) + bg_ref[...]
    o_ref[...] = a * _sigmoid(g)


def _glu_matmul(x2d, w, bias, tm=512, tn=512):
    m, k = x2d.shape
    n = w.shape[1] // 2
    tm, tn = min(tm, m), min(tn, n)
    nt = n // tn
    bias2 = bias.reshape(1, 2 * n)
    return pl.pallas_call(
        _glu_kernel,
        out_shape=jax.ShapeDtypeStruct((m, n), F32),
        grid=(m // tm, nt),
        in_specs=[
            pl.BlockSpec((tm, k), lambda i, j: (i, 0)),
            pl.BlockSpec((k, tn), lambda i, j: (0, j)),
            pl.BlockSpec((k, tn), lambda i, j: (0, j + nt)),
            pl.BlockSpec((1, tn), lambda i, j: (0, j)),
            pl.BlockSpec((1, tn), lambda i, j: (0, j + nt)),
        ],
        out_specs=pl.BlockSpec((tm, tn), lambda i, j: (i, j)),
        compiler_params=_params(("parallel", "parallel")),
        name="glu_matmul",
    )(x2d, w, w, bias2, bias2)


def _layer_norm_rows(v, g, b):
    mu = jnp.mean(v, axis=-1, keepdims=True)
    cen = v - mu
    var = jnp.mean(cen * cen, axis=-1, keepdims=True)
    return cen * lax.rsqrt(var + LN_EPS) * g + b


def _pack_bf16_pairs(h):
    half = h.shape[1] // 2
    hb = h.astype(BF16).astype(F32)
    lo = pltpu.bitcast(hb[:, :half], jnp.uint32)
    hi = pltpu.bitcast(hb[:, half:], jnp.uint32)
    return (lo >> 16) | (hi & jnp.uint32(0xFFFF0000))


def _unpack_f32_half(w, part):
    bits = (w << 16) if part == 0 else (w & jnp.uint32(0xFFFF0000))
    return pltpu.bitcast(bits, F32)


def _unpack_bf16_pairs(w):
    lo = pltpu.bitcast(w << 16, F32).astype(BF16)
    hi = pltpu.bitcast(w & jnp.uint32(0xFFFF0000), F32).astype(BF16)
    return lo, hi


def _route(logits):
    lane = lax.broadcasted_iota(jnp.int32, logits.shape, 1).astype(F32)
    neg = -jnp.inf
    big = float(LANES)
    is_g = lane < N_GROUPS
    gl = jnp.where(is_g, logits, neg)
    gmax = jnp.max(gl, axis=-1, keepdims=True)
    gidx = jnp.min(jnp.where(gl == gmax, lane, big), axis=-1, keepdims=True)
    gsum = jnp.sum(jnp.where(is_g, jnp.exp(gl - gmax), 0.0), axis=-1, keepdims=True)
    gw = 1.0 / gsum
    lo = N_GROUPS + EXPERTS_PER_GROUP * gidx
    el = jnp.where(lane >= lo, jnp.where(lane < lo + EXPERTS_PER_GROUP, logits, neg), neg)
    t1 = jnp.max(el, axis=-1, keepdims=True)
    i1 = jnp.min(jnp.where(el == t1, lane, big), axis=-1, keepdims=True)
    el2 = jnp.where(lane == i1, neg, el)
    t2 = jnp.max(el2, axis=-1, keepdims=True)
    i2 = jnp.min(jnp.where(el2 == t2, lane, big), axis=-1, keepdims=True)
    e = jnp.exp(t2 - t1)
    w1 = gw / (1.0 + e)
    w2 = gw * e / (1.0 + e)
    out = jnp.where(lane == 0.0, i1 - N_GROUPS, 0.0)
    out = jnp.where(lane == 1.0, i2 - N_GROUPS, out)
    out = jnp.where(lane == 2.0, w1, out)
    out = jnp.where(lane == 3.0, w2, out)
    return out


def _ln_router_kernel(x_ref, y_ref, g_ref, lng_ref, lnb_ref, sc_ref, sh_ref, wr_ref, br_ref,
                      xo_ref, hp_ref, route_ref):
    v = ALPHA * x_ref[...] + g_ref[...] * y_ref[...]
    xn = _layer_norm_rows(v, lng_ref[...], lnb_ref[...])
    xo_ref[...] = xn
    h = xn * (1.0 + sc_ref[...]) + sh_ref[...]
    hp_ref[...] = _pack_bf16_pairs(h)
    h_hi = h.astype(BF16)
    h_lo = (h - h_hi.astype(F32)).astype(BF16)
    wr = wr_ref[...]
    w_hi = wr.astype(BF16)
    w_lo = (wr - w_hi.astype(F32)).astype(BF16)
    logits = _dot(h_hi, w_hi) + _dot(h_lo, w_hi) + _dot(h_hi, w_lo) + br_ref[...]
    route_ref[...] = _route(logits)


def _ln_router(x, y, gate, lng, lnb, sc, sh, wr, br, tm=256):
    b, s, d = x.shape
    tm = min(tm, s)
    row = lambda: pl.BlockSpec((None, tm, d), lambda bi, i: (bi, i, 0))
    per_b = lambda: pl.BlockSpec((None, 1, d), lambda bi, i: (bi, 0, 0))
    const = lambda shape: pl.BlockSpec(shape, lambda bi, i: (0,) * len(shape))
    return pl.pallas_call(
        _ln_router_kernel,
        out_shape=(
            jax.ShapeDtypeStruct((b, s, d), F32),
            jax.ShapeDtypeStruct((b, s, d // 2), jnp.uint32),
            jax.ShapeDtypeStruct((b, s, LANES), F32),
        ),
        grid=(b, s // tm),
        in_specs=[row(), row(), per_b(), const((1, d)), const((1, d)), per_b(), per_b(),
                  const((d, LANES)), const((1, LANES))],
        out_specs=(
            row(),
            pl.BlockSpec((None, tm, d // 2), lambda bi, i: (bi, i, 0)),
            pl.BlockSpec((None, tm, LANES), lambda bi, i: (bi, i, 0)),
        ),
        compiler_params=_params(("parallel", "parallel")),
        name="ln_router",
    )(x, y, gate, lng, lnb, sc, sh, wr, br)


def _moe_kernel(be_ref, tok_ref, nu_ref, first_ref, next_ref, h_hbm, wg_hbm, wu_hbm, wd_hbm, o_ref,
                xb_ref, sem, sg_ref, su_ref, sd_ref, wsem, wg_ref, wu_ref, wd_ref, *, layer):
    i = pl.program_id(0)
    n_used = nu_ref[0]
    half = xb_ref.shape[2]
    w_hbm = (wg_hbm, wu_hbm, wd_hbm)
    stage_refs = (sg_ref, su_ref, sd_ref)
    w_work = (wg_ref, wu_ref, wd_ref)

    def weight_copy(k, expert):
        return pltpu.make_async_copy(w_hbm[k].at[layer, expert], stage_refs[k], wsem.at[k])

    def fetch(expert):
        for k in range(3):
            weight_copy(k, expert).start(priority=1)

    def land(expert):
        for k in range(3):
            weight_copy(k, expert).wait()
            w_work[k][...] = stage_refs[k][...].astype(BF16)

    @pl.when(i == 0)
    def _():
        fetch(be_ref[0])

    @pl.when(jnp.logical_and(i < n_used, first_ref[i] == 1))
    def _():
        land(be_ref[i])
        nxt = next_ref[i]

        @pl.when(nxt >= 0)
        def _():
            fetch(nxt)

    def row_copy(tok, slot, r):
        return pltpu.make_async_copy(h_hbm.at[pl.ds(tok, 1)], xb_ref.at[slot, pl.ds(r, 1)], sem.at[slot])

    def issue(blk, slot):
        base = blk * MOE_BLOCK
        for r in range(MOE_BLOCK):
            row_copy(tok_ref[base + r], slot, r).start()

    def wait_all(slot):
        for r in range(MOE_BLOCK):
            row_copy(0, slot, r).wait()

    @pl.when(i == 0)
    def _():
        issue(0, 0)

    @pl.when(i + 1 < n_used)
    def _():
        issue(i + 1, (i + 1) % 2)

    @pl.when(i < n_used)
    def _():
        slot = i % 2
        wait_all(slot)
        lo, hi = _unpack_bf16_pairs(xb_ref[slot])
        gate = _dot(lo, wg_ref[:half, :]) + _dot(hi, wg_ref[half:, :])
        up = _dot(lo, wu_ref[:half, :]) + _dot(hi, wu_ref[half:, :])
        act = (gate * _sigmoid(gate) * up).astype(BF16)
        o_ref[...] = _pack_bf16_pairs(_dot(act, wd_ref[...]))

    @pl.when(i >= n_used)
    def _():
        o_ref[...] = jnp.zeros(o_ref.shape, o_ref.dtype)


def _moe_experts(hp2d, plan, layer, wg, wu, wd):
    block_e, buf_tok, n_used, first, nxt = plan
    n_blocks = block_e.shape[0]
    half = hp2d.shape[1]
    d = 2 * half
    de = wg.shape[3]
    any_spec = pl.BlockSpec(memory_space=pl.ANY)
    grid_spec = pltpu.PrefetchScalarGridSpec(
        num_scalar_prefetch=5,
        grid=(n_blocks,),
        in_specs=[any_spec, any_spec, any_spec, any_spec],
        out_specs=pl.BlockSpec((MOE_BLOCK, half), lambda i, *_: (i, 0)),
        scratch_shapes=[
            pltpu.VMEM((2, MOE_BLOCK, half), jnp.uint32), pltpu.SemaphoreType.DMA((2,)),
            pltpu.VMEM((d, de), F32), pltpu.VMEM((d, de), F32), pltpu.VMEM((de, d), F32),
            pltpu.SemaphoreType.DMA((3,)),
            pltpu.VMEM((d, de), BF16), pltpu.VMEM((d, de), BF16), pltpu.VMEM((de, d), BF16),
        ],
    )
    return pl.pallas_call(
        functools.partial(_moe_kernel, layer=layer),
        out_shape=jax.ShapeDtypeStruct((n_blocks * MOE_BLOCK, half), jnp.uint32),
        grid_spec=grid_spec,
        compiler_params=_params(("arbitrary",)),
        name="moe_experts",
    )(block_e, buf_tok, n_used, first, nxt, hp2d, wg, wu, wd)


def _combine_kernel(pos_ref, ys_hbm, route_ref, x_ref, g_ref, lng_ref, lnb_ref, *rest, tm, with_h):
    if with_h:
        sc_ref, sh_ref, xo_ref, ho_ref, yb_ref, sem = rest
    else:
        xo_ref, yb_ref, sem = rest
    nt = pl.num_programs(1)
    step = pl.program_id(0) * nt + pl.program_id(1)
    total = pl.num_programs(0) * nt

    def row_copy(p, slot, k, r):
        return pltpu.make_async_copy(ys_hbm.at[pl.ds(p, 1)], yb_ref.at[slot, k, pl.ds(r, 1)], sem.at[slot])

    def issue(st, slot):
        base = TOP_K * st * tm
        for r in range(tm):
            for k in range(TOP_K):
                row_copy(pos_ref[base + TOP_K * r + k], slot, k, r).start()

    def wait_all(slot):
        for r in range(tm):
            for k in range(TOP_K):
                row_copy(0, slot, k, r).wait()

    @pl.when(step == 0)
    def _():
        issue(0, 0)

    @pl.when(step + 1 < total)
    def _():
        issue(step + 1, (step + 1) % 2)

    slot = step % 2
    wait_all(slot)
    route = route_ref[...]
    half = yb_ref.shape[3]
    g = g_ref[...]
    x = x_ref[...]
    v_halves = []
    for part in range(2):
        cols = slice(part * half, (part + 1) * half)
        y0, y1 = [_unpack_f32_half(yb_ref[slot, k], part) for k in range(TOP_K)]
        f = route[:, 2:3] * y0 + route[:, 3:4] * y1
        v_halves.append(ALPHA * x[:, cols] + g[:, cols] * f)
    v = jnp.concatenate(v_halves, axis=1)
    xn = _layer_norm_rows(v, lng_ref[...], lnb_ref[...])
    xo_ref[...] = xn
    if with_h:
        ho_ref[...] = (xn * (1.0 + sc_ref[...]) + sh_ref[...]).astype(ho_ref.dtype)


def _moe_combine(ys, pos, route, x, gate, lng, lnb, next_mod=None, tm=256):
    b, s, d = x.shape
    tm = min(tm, s)
    with_h = next_mod is not None
    row = lambda w: pl.BlockSpec((None, tm, w), lambda bi, i, pos: (bi, i, 0))
    per_b = lambda: pl.BlockSpec((None, 1, d), lambda bi, i, pos: (bi, 0, 0))
    const = lambda: pl.BlockSpec((1, d), lambda bi, i, pos: (0, 0))
    in_specs = [pl.BlockSpec(memory_space=pl.ANY), row(LANES), row(d), per_b(), const(), const()]
    out_specs, out_shape = [row(d)], [jax.ShapeDtypeStruct((b, s, d), F32)]
    args = [pos, ys, route, x, gate, lng, lnb]
    if with_h:
        in_specs += [per_b(), per_b()]
        out_specs.append(row(d))
        out_shape.append(jax.ShapeDtypeStruct((b, s, d), BF16))
        args += list(next_mod)
    grid_spec = pltpu.PrefetchScalarGridSpec(
        num_scalar_prefetch=1,
        grid=(b, s // tm),
        in_specs=in_specs,
        out_specs=tuple(out_specs),
        scratch_shapes=[pltpu.VMEM((2, TOP_K, tm, d // 2), jnp.uint32), pltpu.SemaphoreType.DMA((2,))],
    )
    kern = functools.partial(_combine_kernel, tm=tm, with_h=with_h)
    return pl.pallas_call(
        kern,
        out_shape=tuple(out_shape),
        grid_spec=grid_spec,
        compiler_params=_params(("arbitrary", "arbitrary")),
        name="moe_combine",
    )(*args)


def _conv_kernel(prev_ref, cur_ref, next_ref, w_ref, b_ref, g_ref, beta_ref, o_ref, buf_ref, acc_ref, sh_ref,
                 *, ts, tiles_per_seq, rs, cw):
    win_rows = sh_ref.shape[2]
    i = pl.program_id(0)
    first = (i % tiles_per_seq) == 0
    last = (i % tiles_per_seq) == tiles_per_seq - 1
    buf_ref[0:HALO, :] = jnp.where(first, 0.0, prev_ref[...])
    buf_ref[HALO:HALO + ts, :] = cur_ref[...]
    buf_ref[HALO + ts:, :] = jnp.where(last, 0.0, next_ref[...])
    d = cur_ref.shape[1]
    off = HALO - CONV_PAD

    def chunk(ci, carry):
        c0 = pl.multiple_of(ci * cw, cw)
        slot = ci % 2
        for r0 in range(0, ts, rs):
            for phase in range(1, SUBLANES):
                sh_ref[slot, phase] = buf_ref[r0 + phase:r0 + phase + win_rows, pl.ds(c0, cw)]
            acc = jnp.zeros((rs, cw), F32)
            for k in range(CONV_WIDTH):
                j = k + off
                phase, base = j % SUBLANES, j - j % SUBLANES
                if phase == 0:
                    tap = buf_ref[r0 + j:r0 + j + rs, pl.ds(c0, cw)]
                else:
                    tap = sh_ref[slot, phase, base:base + rs, :]
                acc = acc + w_ref[k:k + 1, pl.ds(c0, cw)] * tap
            acc_ref[r0:r0 + rs, pl.ds(c0, cw)] = acc
        return carry

    lax.fori_loop(0, d // cw, chunk, 0)
    u = acc_ref[...] + b_ref[...]
    z = _layer_norm_rows(u, g_ref[...], beta_ref[...])
    o_ref[...] = (z * _sigmoid(z)).astype(o_ref.dtype)


def _conv_ln_swish(u2d, seq_len, w_dw, b_dw, ln_g, ln_b, ts=256, rs=128, cw=128):
    n, d = u2d.shape
    ts = min(ts, seq_len)
    rs = min(rs, ts)
    tiles_per_seq = seq_len // ts
    hb = ts // HALO
    n_halo = n // HALO
    w_pad = jnp.zeros((CONV_WIDTH + 1, d), F32).at[:CONV_WIDTH].set(w_dw)
    last_off = HALO - CONV_PAD + CONV_WIDTH - 1
    max_base = last_off - last_off % SUBLANES
    assert max_base + SUBLANES <= 2 * HALO
    kern = functools.partial(_conv_kernel, ts=ts, tiles_per_seq=tiles_per_seq, rs=rs, cw=cw)
    const = lambda r: pl.BlockSpec((r, d), lambda i: (0, 0))
    return pl.pallas_call(
        kern,
        out_shape=jax.ShapeDtypeStruct((n, d), BF16),
        grid=(n // ts,),
        in_specs=[
            pl.BlockSpec((HALO, d), lambda i: (jnp.maximum(i * hb - 1, 0), 0)),
            pl.BlockSpec((ts, d), lambda i: (i, 0)),
            pl.BlockSpec((HALO, d), lambda i: (jnp.minimum((i + 1) * hb, n_halo - 1), 0)),
            const(CONV_WIDTH + 1), const(1), const(1), const(1),
        ],
        out_specs=pl.BlockSpec((ts, d), lambda i: (i, 0)),
        scratch_shapes=[pltpu.VMEM((ts + 2 * HALO, d), F32), pltpu.VMEM((ts, d), F32),
                        pltpu.VMEM((2, SUBLANES, rs + max_base, cw), F32)],
        compiler_params=_params(("parallel",)),
        name="conv_ln_swish",
    )(u2d, u2d, u2d, w_pad, b_dw.reshape(1, d), ln_g.reshape(1, d), ln_b.reshape(1, d))


def _rope_tables(seq_len):
    inv = ROPE_THETA ** (-jnp.arange(N_FREQ, dtype=F32) / N_FREQ)
    n_rows = seq_len // GRID_W
    row = jnp.repeat(jnp.arange(n_rows, dtype=F32), GRID_W)
    col = jnp.tile(jnp.arange(GRID_W, dtype=F32), n_rows)
    ang_r, ang_c = row[:, None] * inv, col[:, None] * inv
    cr, sr, cc, sc = jnp.cos(ang_r), jnp.sin(ang_r), jnp.cos(ang_c), jnp.sin(ang_c)
    cos = jnp.concatenate([cr, cc, cr, cc], axis=-1)
    sin = jnp.concatenate([-sr, -sc, sr, sc], axis=-1)
    return cos, sin


def _routing_plan(route2d):
    n = route2d.shape[0]
    a = n * TOP_K
    eid_f = route2d[:, :TOP_K].astype(jnp.int32).reshape(a)
    experts = jnp.arange(N_EXPERTS, dtype=jnp.int32)
    counts = jnp.sum(eid_f[:, None] == experts[None, :], axis=0, dtype=jnp.int32)
    pcounts = (counts + MOE_BLOCK - 1) // MOE_BLOCK * MOE_BLOCK
    starts = jnp.cumsum(counts) - counts
    pends = jnp.cumsum(pcounts)
    pstarts = pends - pcounts
    order = jnp.argsort(eid_f)
    rank = jnp.argsort(order)
    pos = (pstarts[eid_f] + rank - starts[eid_f]).astype(jnp.int32)
    n_blocks = -(-a // MOE_BLOCK) + N_EXPERTS
    blk_start = jnp.arange(n_blocks, dtype=jnp.int32) * MOE_BLOCK
    block_e = jnp.minimum(jnp.sum(pends[None, :] <= blk_start[:, None], axis=1, dtype=jnp.int32), N_EXPERTS - 1)
    e_slot = jnp.repeat(block_e, MOE_BLOCK)
    j = jnp.arange(n_blocks * MOE_BLOCK, dtype=jnp.int32) - pstarts[e_slot]
    src = jnp.clip(starts[e_slot] + j, 0, a - 1)
    buf_tok = jnp.where(j < counts[e_slot], order[src] // TOP_K, 0).astype(jnp.int32)
    n_used = (pends[-1:] // MOE_BLOCK).astype(jnp.int32)
    first = jnp.concatenate([jnp.ones((1,), jnp.int32), (block_e[1:] != block_e[:-1]).astype(jnp.int32)])
    later = jnp.where((experts[None, :] > experts[:, None]) & (counts[None, :] > 0), experts[None, :], N_EXPERTS)
    next_expert = jnp.min(later, axis=1)
    nxt = jnp.where(next_expert < N_EXPERTS, next_expert, -1)[block_e].astype(jnp.int32)
    return (block_e, buf_tok, n_used, first, nxt), pos


def _moe_layer(x, y, gate1, lng1, lnb1, sc2, sh2, gate2, lng2, lnb2, next_mod,
               w_rg, b_rg, w_re, b_re, layer, expert_weights):
    b, s, d = x.shape
    n = b * s
    pad = LANES - N_GROUPS - N_EXPERTS
    wr = jnp.concatenate([w_rg, w_re, jnp.zeros((d, pad), F32)], axis=1)
    br = jnp.concatenate([b_rg, b_re, jnp.zeros((pad,), F32)]).reshape(1, LANES)
    x1, hp, route = _ln_router(x, y.reshape(b, s, d), gate1, lng1, lnb1, sc2, sh2, wr, br)
    plan, pos = _routing_plan(route.reshape(n, LANES))
    ys = _moe_experts(hp.reshape(n, d // 2), plan, layer, *expert_weights)
    return _moe_combine(ys, pos, route, x1, gate2, lng2, lnb2, next_mod)


def kernel(x, c, ctx, c_ctx, w_ada, b_ada, ln_g, ln_b, w_qkv, q_gain, k_gain, w_o, w_pw1, b_pw1, w_dw, b_dw,
           conv_ln_g, conv_ln_b, w_pw2, w_rg, b_rg, w_re, b_re, w_gate, w_up, w_down):
    b, s, d = x.shape
    cl = ctx.shape[1]
    n = b * s
    n_heads = d // HEAD_DIM
    n_kv = n_heads // GQA_GROUP
    q_dim, kv_dim = n_heads * HEAD_DIM, n_kv * HEAD_DIM
    assert w_ada.shape[0] == DEPTH and b + 1 <= ADA_ROWS

    cond = jnp.zeros((ADA_ROWS, d), F32).at[:b].set(c).at[b].set(c_ctx)
    mods = _ada_modulation(cond, w_ada, b_ada)

    def mod_chunks(layer, rows):
        m = mods[layer, rows].reshape(-1, 6, d)
        return [m[:, k][:, None, :] for k in range(6)]

    row = lambda v: v.reshape(1, d)

    sh1, sc1, g1, sh2, sc2, g2 = mod_chunks(0, slice(0, b))
    csh1, csc1 = [jnp.broadcast_to(m, (b, 1, d)) for m in mod_chunks(0, slice(b, b + 1))[:2]]
    nsh1, nsc1, ng1, nsh2, nsc2, ng2 = mod_chunks(1, slice(0, b))

    h = _modulate(x, sc1, sh1)
    hc = _modulate(ctx, csc1, csh1)
    wqk = _pair_halves(w_qkv[0][:, :q_dim + kv_dim]).astype(BF16)
    wv = w_qkv[0][:, q_dim + kv_dim:].astype(BF16)
    gain = _pair_halves(jnp.concatenate([jnp.tile(q_gain[0], n_heads), jnp.tile(k_gain[0], n_kv)])).reshape(1, -1)
    scale = jnp.concatenate([jnp.full((q_dim,), ATTN_SCALE * LOG2_E, F32), jnp.ones((kv_dim,), F32)]).reshape(1, -1)
    cos, sin = _rope_tables(s)
    h2d, hc2d = h.reshape(n, d), hc.reshape(b * cl, d)
    qk = _qk_project(h2d, wqk, gain, scale, cos, sin, col_off_cols=0, n_cols=q_dim + kv_dim)
    v = _matmul(h2d, wv, tm=1024, out_dtype=BF16)
    one_c, zero_c = jnp.ones((cl, HEAD_DIM), F32), jnp.zeros((cl, HEAD_DIM), F32)
    kc = _qk_project(hc2d, wqk, gain, scale, one_c, zero_c, col_off_cols=q_dim, n_cols=kv_dim, tm=cl)
    vc = _matmul(hc2d, wv, tm=cl, out_dtype=BF16)
    qk = qk.reshape(b, s, -1)
    k_all = jnp.concatenate([qk[:, :, q_dim:], kc.reshape(b, cl, kv_dim)], axis=1)
    v_all = jnp.concatenate([v.reshape(b, s, n_kv, HEAD_DIM), vc.reshape(b, cl, n_kv, HEAD_DIM)], axis=1)
    sum_cols = jnp.zeros((b, s + cl, n_kv, V_EXTRA), BF16).at[..., 0].set(1.0)
    vt_all = jnp.concatenate([v_all, sum_cols], axis=-1).transpose(0, 2, 3, 1).reshape(b, n_kv * V_ROWS, s + cl)
    o = _attention(qk, k_all, vt_all, n_heads=n_heads, n_kv=n_kv)
    y = _matmul(o.reshape(n, d), w_o[0].astype(BF16))
    x, hn = _moe_layer(x, y, g1, row(ln_g[0, 0]), row(ln_b[0, 0]), sc2, sh2, g2, row(ln_g[0, 1]), row(ln_b[0, 1]),
                       (nsc1, nsh1), w_rg[0], b_rg[0], w_re[0], b_re[0],
                       0, (w_gate, w_up, w_down))

    u = _glu_matmul(hn.reshape(n, d), w_pw1[0].astype(BF16), b_pw1[0])
    cv = _conv_ln_swish(u, s, w_dw[0], b_dw[0], conv_ln_g[0], conv_ln_b[0])
    y = _matmul(cv, w_pw2[0].astype(BF16))
    (x,) = _moe_layer(x, y, ng1, row(ln_g[1, 0]), row(ln_b[1, 0]), nsc2, nsh2, ng2, row(ln_g[1, 1]), row(ln_b[1, 1]),
                      None, w_rg[1], b_rg[1], w_re[1], b_re[1], 1, (w_gate, w_up, w_down))
    return x
```

```python
import functools
import math

import jax
import jax.numpy as jnp
from jax import lax
from jax.experimental import pallas as pl
from jax.experimental.pallas import tpu as pltpu

F32 = jnp.float32
BF16 = jnp.bfloat16

HEAD_DIM = 128
GQA_GROUP = 4
GRID_W = 64
ROPE_THETA = 10000.0
N_FREQ = HEAD_DIM // 4
ATTN_SCALE = HEAD_DIM ** -0.5
LOG2_E = 1.4426950408889634
CONV_WIDTH = 31
CONV_PAD = CONV_WIDTH // 2
N_GROUPS = 4
EXPERTS_PER_GROUP = 8
N_EXPERTS = N_GROUPS * EXPERTS_PER_GROUP
TOP_K = 2
MOE_BLOCK = 256
LN_EPS = 1e-5
QK_EPS = 1e-6
DEPTH = 2
ALPHA = (2 * DEPTH) ** 0.25

ATTN_BUFFERS = 3
V_EXTRA = 16
V_ROWS = HEAD_DIM + V_EXTRA
LANES = 128
SUBLANES = 8
HALO = 16
ADA_ROWS = 8
VMEM_LIMIT = 56 * 1024 * 1024


def _params(sem, vmem=VMEM_LIMIT, flags=None):
    return pltpu.CompilerParams(dimension_semantics=sem, vmem_limit_bytes=vmem, flags=flags)


def _sigmoid(x):
    return 1.0 / (1.0 + jnp.exp(-x))


def _dot(a, b):
    return jnp.dot(a, b, preferred_element_type=F32)


def _ada_kernel(c_ref, w_ref, b_ref, o_ref):
    c = c_ref[...]
    s = (c * _sigmoid(c)).astype(BF16)
    o_ref[...] = _dot(s, w_ref[...].astype(BF16)) + b_ref[...]


def _ada_modulation(cond, w_ada, b_ada, tn=512):
    depth, d, n6 = w_ada.shape
    return pl.pallas_call(
        _ada_kernel,
        out_shape=jax.ShapeDtypeStruct((depth, ADA_ROWS, n6), F32),
        grid=(depth, n6 // tn),
        in_specs=[
            pl.BlockSpec((ADA_ROWS, d), lambda l, j: (0, 0)),
            pl.BlockSpec((None, d, tn), lambda l, j: (l, 0, j)),
            pl.BlockSpec((None, 1, tn), lambda l, j: (l, 0, j)),
        ],
        out_specs=pl.BlockSpec((None, ADA_ROWS, tn), lambda l, j: (l, 0, j)),
        compiler_params=_params(("parallel", "parallel")),
        name="ada_modulation",
    )(cond, w_ada, b_ada.reshape(depth, 1, n6))


def _modulate_kernel(x_ref, sc_ref, sh_ref, o_ref):
    o_ref[...] = (x_ref[...] * (1.0 + sc_ref[...]) + sh_ref[...]).astype(o_ref.dtype)


def _modulate(x, sc, sh, tm=256):
    b, s, d = x.shape
    tm = min(tm, s)
    return pl.pallas_call(
        _modulate_kernel,
        out_shape=jax.ShapeDtypeStruct((b, s, d), BF16),
        grid=(b, s // tm),
        in_specs=[
            pl.BlockSpec((None, tm, d), lambda bi, i: (bi, i, 0)),
            pl.BlockSpec((None, 1, d), lambda bi, i: (bi, 0, 0)),
            pl.BlockSpec((None, 1, d), lambda bi, i: (bi, 0, 0)),
        ],
        out_specs=pl.BlockSpec((None, tm, d), lambda bi, i: (bi, i, 0)),
        compiler_params=_params(("parallel", "parallel")),
        name="modulate",
    )(x, sc, sh)


def _qk_kernel(x_ref, w_ref, gain_ref, scale_ref, cos_ref, sin_ref, o_ref, y_a, y_b):
    u = pl.program_id(0)

    @pl.when(u == 0)
    def _():
        y_b[...] = jnp.zeros(y_b.shape, F32)

    def step(y_new, y_old):
        y_new[...] = _dot(x_ref[...], w_ref[...])
        cos = cos_ref[...]
        sin = sin_ref[...]
        for h in range(o_ref.shape[1] // HEAD_DIM):
            sl = slice(h * HEAD_DIM, (h + 1) * HEAD_DIM)
            yh = y_old[:, sl]
            ms = jnp.mean(yh * yh, axis=-1, keepdims=True)
            yn = yh * lax.rsqrt(ms + QK_EPS) * gain_ref[:, sl]
            rot = yn * cos + pltpu.roll(yn, HEAD_DIM // 2, 1) * sin
            o_ref[:, sl] = (rot * scale_ref[:, sl]).astype(o_ref.dtype)

    @pl.when(u % 2 == 0)
    def _():
        step(y_a, y_b)

    @pl.when(u % 2 == 1)
    def _():
        step(y_b, y_a)


def _pair_halves(a):
    lead = a.shape[:-1]
    a = a.reshape(lead + (a.shape[-1] // HEAD_DIM, 2, 2, N_FREQ))
    return jnp.swapaxes(a, -2, -3).reshape(lead + (-1,))


def _qk_project(x2d, w, gain, scale, cos, sin, *, col_off_cols, n_cols, tm=512, tn=1024):
    m, d = x2d.shape
    tm = min(tm, m, cos.shape[0])
    tn = math.gcd(tn, n_cols, col_off_cols)
    pos_tiles = cos.shape[0] // tm
    col_off = col_off_cols // tn
    mi, nj = m // tm, n_cols // tn
    n_steps = mi * nj

    def mm_tile(u):
        t = jnp.minimum(u, n_steps - 1)
        return t % mi, t // mi

    def ep_tile(u):
        t = jnp.maximum(u - 1, 0)
        return t % mi, t // mi

    return pl.pallas_call(
        _qk_kernel,
        out_shape=jax.ShapeDtypeStruct((m, n_cols), BF16),
        grid=(n_steps + 1,),
        in_specs=[
            pl.BlockSpec((tm, d), lambda u: (mm_tile(u)[0], 0)),
            pl.BlockSpec((d, tn), lambda u: (0, mm_tile(u)[1] + col_off)),
            pl.BlockSpec((1, tn), lambda u: (0, ep_tile(u)[1] + col_off)),
            pl.BlockSpec((1, tn), lambda u: (0, ep_tile(u)[1] + col_off)),
            pl.BlockSpec((tm, HEAD_DIM), lambda u: (ep_tile(u)[0] % pos_tiles, 0)),
            pl.BlockSpec((tm, HEAD_DIM), lambda u: (ep_tile(u)[0] % pos_tiles, 0)),
        ],
        out_specs=pl.BlockSpec((tm, tn), lambda u: ep_tile(u)),
        scratch_shapes=[pltpu.VMEM((tm, tn), F32), pltpu.VMEM((tm, tn), F32)],
        compiler_params=_params(("arbitrary",)),
        name="qk_project",
    )(x2d, w, gain, scale, cos, sin)


def _attn_kernel(q_ref, k_ref, vt_ref, o_ref, qt_ref, s_bufs, p_bufs, al_bufs, m_ref, acc_ref,
                 *, tq, tk, n_chunks, rg):
    for g in range(GQA_GROUP):
        qt_ref[:, g * tq:(g + 1) * tq] = q_ref[:, g * HEAD_DIM:(g + 1) * HEAD_DIM].T
    m_ref[...] = jnp.full(m_ref.shape, -jnp.inf, F32)
    acc_ref[...] = jnp.zeros(acc_ref.shape, F32)

    def start(t):
        return t * tk

    def scores(t, par):
        s_bufs[par] = _dot(k_ref[pl.ds(start(t), tk), :], qt_ref[...])

    def softmax(par):
        mx = s_bufs[par, 0:rg, :]
        for r in range(rg, tk, rg):
            mx = jnp.maximum(mx, s_bufs[par, r:r + rg, :])
        m_prev = m_ref[...]
        m_new = jnp.maximum(m_prev, jnp.max(mx, axis=0, keepdims=True))
        for r in range(0, tk, rg):
            p_bufs[par, r:r + rg, :] = jnp.exp2(s_bufs[par, r:r + rg, :] - m_new).astype(BF16)
        m_ref[...] = m_new
        al_bufs[par] = jnp.exp2(m_prev - m_new)

    def values(t, par):
        acc_ref[...] = al_bufs[par] * acc_ref[...] + _dot(vt_ref[:, pl.ds(start(t), tk)], p_bufs[par])

    n_buf = s_bufs.shape[0]
    scores(0, 0)
    for t in range(n_chunks):
        if t + 1 < n_chunks:
            scores(t + 1, (t + 1) % n_buf)
        softmax(t % n_buf)
        if t >= 1:
            values(t - 1, (t - 1) % n_buf)
    values(n_chunks - 1, (n_chunks - 1) % n_buf)

    out = (acc_ref[:HEAD_DIM, :] / acc_ref[HEAD_DIM:HEAD_DIM + 1, :]).T
    for g in range(GQA_GROUP):
        o_ref[:, g * HEAD_DIM:(g + 1) * HEAD_DIM] = out[g * tq:(g + 1) * tq, :].astype(o_ref.dtype)


def _attention(q_src, k_all, vt_all, *, n_heads, n_kv, tq=256, tk=768):
    b, s, _ = q_src.shape
    sk = k_all.shape[1]
    tq = min(tq, s)
    assert sk % tk == 0 and s % tq == 0
    gw = GQA_GROUP * HEAD_DIM
    rows = GQA_GROUP * tq
    kern = functools.partial(_attn_kernel, tq=tq, tk=tk, n_chunks=sk // tk, rg=16)
    return pl.pallas_call(
        kern,
        out_shape=jax.ShapeDtypeStruct((b, s, n_heads * HEAD_DIM), BF16),
        grid=(b, n_kv, s // tq),
        in_specs=[
            pl.BlockSpec((None, tq, gw), lambda bi, h, i: (bi, i, h)),
            pl.BlockSpec((None, sk, HEAD_DIM), lambda bi, h, i: (bi, 0, h)),
            pl.BlockSpec((None, V_ROWS, sk), lambda bi, h, i: (bi, h, 0)),
        ],
        out_specs=pl.BlockSpec((None, tq, gw), lambda bi, h, i: (bi, i, h)),
        scratch_shapes=[
            pltpu.VMEM((HEAD_DIM, rows), BF16),
            pltpu.VMEM((ATTN_BUFFERS, tk, rows), F32),
            pltpu.VMEM((ATTN_BUFFERS, tk, rows), BF16),
            pltpu.VMEM((ATTN_BUFFERS, 1, rows), F32),
            pltpu.VMEM((1, rows), F32),
            pltpu.VMEM((V_ROWS, rows), F32),
        ],
        compiler_params=_params(("parallel", "parallel", "arbitrary")),
        name="attention",
    )(q_src, k_all, vt_all)


def _matmul_kernel(x_ref, w_ref, o_ref):
    o_ref[...] = _dot(x_ref[...], w_ref[...]).astype(o_ref.dtype)


def _matmul(x2d, w, tm=1024, tn=1024, col_off_cols=0, n_cols=None, out_dtype=F32):
    m, k = x2d.shape
    n = w.shape[1] if n_cols is None else n_cols
    tm, tn = min(tm, m), math.gcd(tn, n, col_off_cols)
    col_off = col_off_cols // tn
    return pl.pallas_call(
        _matmul_kernel,
        out_shape=jax.ShapeDtypeStruct((m, n), out_dtype),
        grid=(m // tm, n // tn),
        in_specs=[pl.BlockSpec((tm, k), lambda i, j: (i, 0)), pl.BlockSpec((k, tn), lambda i, j: (0, j + col_off))],
        out_specs=pl.BlockSpec((tm, tn), lambda i, j: (i, j)),
        compiler_params=_params(("parallel", "parallel")),
        name="matmul",
    )(x2d, w)


def _glu_kernel(x_ref, wa_ref, wg_ref, ba_ref, bg_ref, o_ref):
    x = x_ref[...]
    a = _dot(x, wa_ref[...]) + ba_ref[...]
    g = _dot(x, wg_ref[...]) + bg_ref[...]
    o_ref[...] = a * _sigmoid(g)


def _glu_matmul(x2d, w, bias, tm=1024, tn=512):
    m, k = x2d.shape
    n = w.shape[1] // 2
    tm, tn = min(tm, m), min(tn, n)
    nt = n // tn
    bias2 = bias.reshape(1, 2 * n)
    return pl.pallas_call(
        _glu_kernel,
        out_shape=jax.ShapeDtypeStruct((m, n), F32),
        grid=(m // tm, nt),
        in_specs=[
            pl.BlockSpec((tm, k), lambda i, j: (i, 0)),
            pl.BlockSpec((k, tn), lambda i, j: (0, j)),
            pl.BlockSpec((k, tn), lambda i, j: (0, j + nt)),
            pl.BlockSpec((1, tn), lambda i, j: (0, j)),
            pl.BlockSpec((1, tn), lambda i, j: (0, j + nt)),
        ],
        out_specs=pl.BlockSpec((tm, tn), lambda i, j: (i, j)),
        compiler_params=_params(("parallel", "parallel")),
        name="glu_matmul",
    )(x2d, w, w, bias2, bias2)


def _layer_norm_rows(v, g, b):
    mu = jnp.mean(v, axis=-1, keepdims=True)
    cen = v - mu
    var = jnp.mean(cen * cen, axis=-1, keepdims=True)
    return cen * lax.rsqrt(var + LN_EPS) * g + b


def _pack_bf16_pairs(h):
    half = h.shape[1] // 2
    hb = h.astype(BF16).astype(F32)
    lo = pltpu.bitcast(hb[:, :half], jnp.uint32)
    hi = pltpu.bitcast(hb[:, half:], jnp.uint32)
    return (lo >> 16) | (hi & jnp.uint32(0xFFFF0000))


def _unpack_f32_half(w, part):
    bits = (w << 16) if part == 0 else (w & jnp.uint32(0xFFFF0000))
    return pltpu.bitcast(bits, F32)


def _unpack_bf16_pairs(w):
    lo = pltpu.bitcast(w << 16, F32).astype(BF16)
    hi = pltpu.bitcast(w & jnp.uint32(0xFFFF0000), F32).astype(BF16)
    return lo, hi


def _route(logits):
    lane = lax.broadcasted_iota(jnp.int32, logits.shape, 1).astype(F32)
    neg = -jnp.inf
    big = float(LANES)
    is_g = lane < N_GROUPS
    gl = jnp.where(is_g, logits, neg)
    gmax = jnp.max(gl, axis=-1, keepdims=True)
    gidx = jnp.min(jnp.where(gl == gmax, lane, big), axis=-1, keepdims=True)
    gsum = jnp.sum(jnp.where(is_g, jnp.exp(gl - gmax), 0.0), axis=-1, keepdims=True)
    gw = 1.0 / gsum
    lo = N_GROUPS + EXPERTS_PER_GROUP * gidx
    el = jnp.where(lane >= lo, jnp.where(lane < lo + EXPERTS_PER_GROUP, logits, neg), neg)
    t1 = jnp.max(el, axis=-1, keepdims=True)
    i1 = jnp.min(jnp.where(el == t1, lane, big), axis=-1, keepdims=True)
    el2 = jnp.where(lane == i1, neg, el)
    t2 = jnp.max(el2, axis=-1, keepdims=True)
    i2 = jnp.min(jnp.where(el2 == t2, lane, big), axis=-1, keepdims=True)
    e = jnp.exp(t2 - t1)
    w1 = gw / (1.0 + e)
    w2 = gw * e / (1.0 + e)
    out = jnp.where(lane == 0.0, i1 - N_GROUPS, 0.0)
    out = jnp.where(lane == 1.0, i2 - N_GROUPS, out)
    out = jnp.where(lane == 2.0, w1, out)
    out = jnp.where(lane == 3.0, w2, out)
    return out


def _ln_router_kernel(x_ref, y_ref, g_ref, lng_ref, lnb_ref, sc_ref, sh_ref, wr_ref, br_ref,
                      xo_ref, hp_ref, route_ref):
    v = ALPHA * x_ref[...] + g_ref[...] * y_ref[...]
    xn = _layer_norm_rows(v, lng_ref[...], lnb_ref[...])
    xo_ref[...] = xn
    h = xn * (1.0 + sc_ref[...]) + sh_ref[...]
    hp_ref[...] = _pack_bf16_pairs(h)
    h_hi = h.astype(BF16)
    h_lo = (h - h_hi.astype(F32)).astype(BF16)
    wr = wr_ref[...]
    w_hi = wr.astype(BF16)
    w_lo = (wr - w_hi.astype(F32)).astype(BF16)
    logits = _dot(h_hi, w_hi) + _dot(h_lo, w_hi) + _dot(h_hi, w_lo) + br_ref[...]
    route_ref[...] = _route(logits)


def _ln_router(x, y, gate, lng, lnb, sc, sh, wr, br, tm=256):
    b, s, d = x.shape
    tm = min(tm, s)
    row = lambda: pl.BlockSpec((None, tm, d), lambda bi, i: (bi, i, 0))
    per_b = lambda: pl.BlockSpec((None, 1, d), lambda bi, i: (bi, 0, 0))
    const = lambda shape: pl.BlockSpec(shape, lambda bi, i: (0,) * len(shape))
    return pl.pallas_call(
        _ln_router_kernel,
        out_shape=(
            jax.ShapeDtypeStruct((b, s, d), F32),
            jax.ShapeDtypeStruct((b, s, d // 2), jnp.uint32),
            jax.ShapeDtypeStruct((b, s, LANES), F32),
        ),
        grid=(b, s // tm),
        in_specs=[row(), row(), per_b(), const((1, d)), const((1, d)), per_b(), per_b(),
                  const((d, LANES)), const((1, LANES))],
        out_specs=(
            row(),
            pl.BlockSpec((None, tm, d // 2), lambda bi, i: (bi, i, 0)),
            pl.BlockSpec((None, tm, LANES), lambda bi, i: (bi, i, 0)),
        ),
        compiler_params=_params(("parallel", "parallel")),
        name="ln_router",
    )(x, y, gate, lng, lnb, sc, sh, wr, br)


def _moe_kernel(be_ref, tok_ref, nu_ref, first_ref, next_ref, h_hbm, wg_hbm, wu_hbm, wd_hbm, o_ref,
                xb_ref, sem, sg_ref, su_ref, sd_ref, wsem, wg_ref, wu_ref, wd_ref, *, layer):
    i = pl.program_id(0)
    n_used = nu_ref[0]
    half = xb_ref.shape[2]
    w_hbm = (wg_hbm, wu_hbm, wd_hbm)
    stage_refs = (sg_ref, su_ref, sd_ref)
    w_work = (wg_ref, wu_ref, wd_ref)

    def weight_copy(k, expert):
        return pltpu.make_async_copy(w_hbm[k].at[layer, expert], stage_refs[k], wsem.at[k])

    def fetch(expert):
        for k in range(3):
            weight_copy(k, expert).start(priority=1)

    def land(expert):
        for k in range(3):
            weight_copy(k, expert).wait()
            w_work[k][...] = stage_refs[k][...].astype(BF16)

    @pl.when(i == 0)
    def _():
        fetch(be_ref[0])

    @pl.when(jnp.logical_and(i < n_used, first_ref[i] == 1))
    def _():
        land(be_ref[i])
        nxt = next_ref[i]

        @pl.when(nxt >= 0)
        def _():
            fetch(nxt)

    def row_copy(tok, slot, r):
        return pltpu.make_async_copy(h_hbm.at[pl.ds(tok, 1)], xb_ref.at[slot, pl.ds(r, 1)], sem.at[slot])

    def issue(blk, slot):
        base = blk * MOE_BLOCK
        for r in range(MOE_BLOCK):
            row_copy(tok_ref[base + r], slot, r).start()

    def wait_all(slot):
        for r in range(MOE_BLOCK):
            row_copy(0, slot, r).wait()

    @pl.when(i == 0)
    def _():
        issue(0, 0)

    @pl.when(i + 1 < n_used)
    def _():
        issue(i + 1, (i + 1) % 2)

    @pl.when(i < n_used)
    def _():
        slot = i % 2
        wait_all(slot)
        lo, hi = _unpack_bf16_pairs(xb_ref[slot])
        gate = _dot(lo, wg_ref[:half, :]) + _dot(hi, wg_ref[half:, :])
        up = _dot(lo, wu_ref[:half, :]) + _dot(hi, wu_ref[half:, :])
        act = (gate * _sigmoid(gate) * up).astype(BF16)
        o_ref[...] = _pack_bf16_pairs(_dot(act, wd_ref[...]))

    @pl.when(i >= n_used)
    def _():
        o_ref[...] = jnp.zeros(o_ref.shape, o_ref.dtype)


def _moe_experts(hp2d, plan, layer, wg, wu, wd):
    block_e, buf_tok, n_used, first, nxt = plan
    n_blocks = block_e.shape[0]
    half = hp2d.shape[1]
    d = 2 * half
    de = wg.shape[3]
    any_spec = pl.BlockSpec(memory_space=pl.ANY)
    grid_spec = pltpu.PrefetchScalarGridSpec(
        num_scalar_prefetch=5,
        grid=(n_blocks,),
        in_specs=[any_spec, any_spec, any_spec, any_spec],
        out_specs=pl.BlockSpec((MOE_BLOCK, half), lambda i, *_: (i, 0)),
        scratch_shapes=[
            pltpu.VMEM((2, MOE_BLOCK, half), jnp.uint32), pltpu.SemaphoreType.DMA((2,)),
            pltpu.VMEM((d, de), F32), pltpu.VMEM((d, de), F32), pltpu.VMEM((de, d), F32),
            pltpu.SemaphoreType.DMA((3,)),
            pltpu.VMEM((d, de), BF16), pltpu.VMEM((d, de), BF16), pltpu.VMEM((de, d), BF16),
        ],
    )
    return pl.pallas_call(
        functools.partial(_moe_kernel, layer=layer),
        out_shape=jax.ShapeDtypeStruct((n_blocks * MOE_BLOCK, half), jnp.uint32),
        grid_spec=grid_spec,
        compiler_params=_params(("arbitrary",)),
        name="moe_experts",
    )(block_e, buf_tok, n_used, first, nxt, hp2d, wg, wu, wd)


def _combine_kernel(pos_ref, ys_hbm, route_ref, x_ref, g_ref, lng_ref, lnb_ref, *rest, tm, with_h):
    if with_h:
        sc_ref, sh_ref, xo_ref, ho_ref, yb_ref, sem = rest
    else:
        xo_ref, yb_ref, sem = rest
    nt = pl.num_programs(1)
    step = pl.program_id(0) * nt + pl.program_id(1)
    total = pl.num_programs(0) * nt

    def row_copy(p, slot, k, r):
        return pltpu.make_async_copy(ys_hbm.at[pl.ds(p, 1)], yb_ref.at[slot, k, pl.ds(r, 1)], sem.at[slot])

    def issue(st, slot):
        base = TOP_K * st * tm
        for r in range(tm):
            for k in range(TOP_K):
                row_copy(pos_ref[base + TOP_K * r + k], slot, k, r).start()

    def wait_all(slot):
        for r in range(tm):
            for k in range(TOP_K):
                row_copy(0, slot, k, r).wait()

    @pl.when(step == 0)
    def _():
        issue(0, 0)

    @pl.when(step + 1 < total)
    def _():
        issue(step + 1, (step + 1) % 2)

    slot = step % 2
    wait_all(slot)
    route = route_ref[...]
    half = yb_ref.shape[3]
    g = g_ref[...]
    x = x_ref[...]
    v_halves = []
    for part in range(2):
        cols = slice(part * half, (part + 1) * half)
        y0, y1 = [_unpack_f32_half(yb_ref[slot, k], part) for k in range(TOP_K)]
        f = route[:, 2:3] * y0 + route[:, 3:4] * y1
        v_halves.append(ALPHA * x[:, cols] + g[:, cols] * f)
    v = jnp.concatenate(v_halves, axis=1)
    xn = _layer_norm_rows(v, lng_ref[...], lnb_ref[...])
    xo_ref[...] = xn
    if with_h:
        ho_ref[...] = (xn * (1.0 + sc_ref[...]) + sh_ref[...]).astype(ho_ref.dtype)


def _moe_combine(ys, pos, route, x, gate, lng, lnb, next_mod=None, tm=256):
    b, s, d = x.shape
    tm = min(tm, s)
    with_h = next_mod is not None
    row = lambda w: pl.BlockSpec((None, tm, w), lambda bi, i, pos: (bi, i, 0))
    per_b = lambda: pl.BlockSpec((None, 1, d), lambda bi, i, pos: (bi, 0, 0))
    const = lambda: pl.BlockSpec((1, d), lambda bi, i, pos: (0, 0))
    in_specs = [pl.BlockSpec(memory_space=pl.ANY), row(LANES), row(d), per_b(), const(), const()]
    out_specs, out_shape = [row(d)], [jax.ShapeDtypeStruct((b, s, d), F32)]
    args = [pos, ys, route, x, gate, lng, lnb]
    if with_h:
        in_specs += [per_b(), per_b()]
        out_specs.append(row(d))
        out_shape.append(jax.ShapeDtypeStruct((b, s, d), BF16))
        args += list(next_mod)
    grid_spec = pltpu.PrefetchScalarGridSpec(
        num_scalar_prefetch=1,
        grid=(b, s // tm),
        in_specs=in_specs,
        out_specs=tuple(out_specs),
        scratch_shapes=[pltpu.VMEM((2, TOP_K, tm, d // 2), jnp.uint32), pltpu.SemaphoreType.DMA((2,))],
    )
    kern = functools.partial(_combine_kernel, tm=tm, with_h=with_h)
    return pl.pallas_call(
        kern,
        out_shape=tuple(out_shape),
        grid_spec=grid_spec,
        compiler_params=_params(("arbitrary", "arbitrary")),
        name="moe_combine",
    )(*args)


def _conv_kernel(prev_ref, cur_ref, next_ref, w_ref, b_ref, g_ref, beta_ref, o_ref, buf_ref, acc_ref, sh_ref,
                 *, ts, tiles_per_seq, rs, cw):
    win_rows = sh_ref.shape[2]
    i = pl.program_id(0)
    first = (i % tiles_per_seq) == 0
    last = (i % tiles_per_seq) == tiles_per_seq - 1
    buf_ref[0:HALO, :] = jnp.where(first, 0.0, prev_ref[...])
    buf_ref[HALO:HALO + ts, :] = cur_ref[...]
    buf_ref[HALO + ts:, :] = jnp.where(last, 0.0, next_ref[...])
    d = cur_ref.shape[1]
    off = HALO - CONV_PAD

    def chunk(ci, carry):
        c0 = pl.multiple_of(ci * cw, cw)
        slot = ci % 2
        for r0 in range(0, ts, rs):
            for phase in range(1, SUBLANES):
                sh_ref[slot, phase] = buf_ref[r0 + phase:r0 + phase + win_rows, pl.ds(c0, cw)]
            acc = jnp.zeros((rs, cw), F32)
            for k in range(CONV_WIDTH):
                j = k + off
                phase, base = j % SUBLANES, j - j % SUBLANES
                if phase == 0:
                    tap = buf_ref[r0 + j:r0 + j + rs, pl.ds(c0, cw)]
                else:
                    tap = sh_ref[slot, phase, base:base + rs, :]
                acc = acc + w_ref[k:k + 1, pl.ds(c0, cw)] * tap
            acc_ref[r0:r0 + rs, pl.ds(c0, cw)] = acc
        return carry

    lax.fori_loop(0, d // cw, chunk, 0)
    u = acc_ref[...] + b_ref[...]
    z = _layer_norm_rows(u, g_ref[...], beta_ref[...])
    o_ref[...] = (z * _sigmoid(z)).astype(o_ref.dtype)


def _conv_ln_swish(u2d, seq_len, w_dw, b_dw, ln_g, ln_b, ts=256, rs=128, cw=128):
    n, d = u2d.shape
    ts = min(ts, seq_len)
    rs = min(rs, ts)
    tiles_per_seq = seq_len // ts
    hb = ts // HALO
    n_halo = n // HALO
    w_pad = jnp.zeros((CONV_WIDTH + 1, d), F32).at[:CONV_WIDTH].set(w_dw)
    last_off = HALO - CONV_PAD + CONV_WIDTH - 1
    max_base = last_off - last_off % SUBLANES
    assert max_base + SUBLANES <= 2 * HALO
    kern = functools.partial(_conv_kernel, ts=ts, tiles_per_seq=tiles_per_seq, rs=rs, cw=cw)
    const = lambda r: pl.BlockSpec((r, d), lambda i: (0, 0))
    return pl.pallas_call(
        kern,
        out_shape=jax.ShapeDtypeStruct((n, d), BF16),
        grid=(n // ts,),
        in_specs=[
            pl.BlockSpec((HALO, d), lambda i: (jnp.maximum(i * hb - 1, 0), 0)),
            pl.BlockSpec((ts, d), lambda i: (i, 0)),
            pl.BlockSpec((HALO, d), lambda i: (jnp.minimum((i + 1) * hb, n_halo - 1), 0)),
            const(CONV_WIDTH + 1), const(1), const(1), const(1),
        ],
        out_specs=pl.BlockSpec((ts, d), lambda i: (i, 0)),
        scratch_shapes=[pltpu.VMEM((ts + 2 * HALO, d), F32), pltpu.VMEM((ts, d), F32),
                        pltpu.VMEM((2, SUBLANES, rs + max_base, cw), F32)],
        compiler_params=_params(("parallel",)),
        name="conv_ln_swish",
    )(u2d, u2d, u2d, w_pad, b_dw.reshape(1, d), ln_g.reshape(1, d), ln_b.reshape(1, d))


def _rope_tables(seq_len):
    inv = ROPE_THETA ** (-jnp.arange(N_FREQ, dtype=F32) / N_FREQ)
    n_rows = seq_len // GRID_W
    row = jnp.repeat(jnp.arange(n_rows, dtype=F32), GRID_W)
    col = jnp.tile(jnp.arange(GRID_W, dtype=F32), n_rows)
    ang_r, ang_c = row[:, None] * inv, col[:, None] * inv
    cr, sr, cc, sc = jnp.cos(ang_r), jnp.sin(ang_r), jnp.cos(ang_c), jnp.sin(ang_c)
    cos = jnp.concatenate([cr, cc, cr, cc], axis=-1)
    sin = jnp.concatenate([-sr, -sc, sr, sc], axis=-1)
    return cos, sin


def _routing_plan(route2d):
    n = route2d.shape[0]
    a = n * TOP_K
    eid_f = route2d[:, :TOP_K].astype(jnp.int32).reshape(a)
    experts = jnp.arange(N_EXPERTS, dtype=jnp.int32)
    counts = jnp.sum(eid_f[:, None] == experts[None, :], axis=0, dtype=jnp.int32)
    pcounts = (counts + MOE_BLOCK - 1) // MOE_BLOCK * MOE_BLOCK
    starts = jnp.cumsum(counts) - counts
    pends = jnp.cumsum(pcounts)
    pstarts = pends - pcounts
    order = jnp.argsort(eid_f)
    rank = jnp.argsort(order)
    pos = (pstarts[eid_f] + rank - starts[eid_f]).astype(jnp.int32)
    n_blocks = -(-a // MOE_BLOCK) + N_EXPERTS
    blk_start = jnp.arange(n_blocks, dtype=jnp.int32) * MOE_BLOCK
    block_e = jnp.minimum(jnp.sum(pends[None, :] <= blk_start[:, None], axis=1, dtype=jnp.int32), N_EXPERTS - 1)
    e_slot = jnp.repeat(block_e, MOE_BLOCK)
    j = jnp.arange(n_blocks * MOE_BLOCK, dtype=jnp.int32) - pstarts[e_slot]
    src = jnp.clip(starts[e_slot] + j, 0, a - 1)
    buf_tok = jnp.where(j < counts[e_slot], order[src] // TOP_K, 0).astype(jnp.int32)
    n_used = (pends[-1:] // MOE_BLOCK).astype(jnp.int32)
    first = jnp.concatenate([jnp.ones((1,), jnp.int32), (block_e[1:] != block_e[:-1]).astype(jnp.int32)])
    later = jnp.where((experts[None, :] > experts[:, None]) & (counts[None, :] > 0), experts[None, :], N_EXPERTS)
    next_expert = jnp.min(later, axis=1)
    nxt = jnp.where(next_expert < N_EXPERTS, next_expert, -1)[block_e].astype(jnp.int32)
    return (block_e, buf_tok, n_used, first, nxt), pos


def _moe_layer(x, y, gate1, lng1, lnb1, sc2, sh2, gate2, lng2, lnb2, next_mod,
               w_rg, b_rg, w_re, b_re, layer, expert_weights):
    b, s, d = x.shape
    n = b * s
    pad = LANES - N_GROUPS - N_EXPERTS
    wr = jnp.concatenate([w_rg, w_re, jnp.zeros((d, pad), F32)], axis=1)
    br = jnp.concatenate([b_rg, b_re, jnp.zeros((pad,), F32)]).reshape(1, LANES)
    x1, hp, route = _ln_router(x, y.reshape(b, s, d), gate1, lng1, lnb1, sc2, sh2, wr, br)
    plan, pos = _routing_plan(route.reshape(n, LANES))
    ys = _moe_experts(hp.reshape(n, d // 2), plan, layer, *expert_weights)
    return _moe_combine(ys, pos, route, x1, gate2, lng2, lnb2, next_mod)


def kernel(x, c, ctx, c_ctx, w_ada, b_ada, ln_g, ln_b, w_qkv, q_gain, k_gain, w_o, w_pw1, b_pw1, w_dw, b_dw,
           conv_ln_g, conv_ln_b, w_pw2, w_rg, b_rg, w_re, b_re, w_gate, w_up, w_down):
    b, s, d = x.shape
    cl = ctx.shape[1]
    n = b * s
    n_heads = d // HEAD_DIM
    n_kv = n_heads // GQA_GROUP
    q_dim, kv_dim = n_heads * HEAD_DIM, n_kv * HEAD_DIM
    assert w_ada.shape[0] == DEPTH and b + 1 <= ADA_ROWS

    cond = jnp.zeros((ADA_ROWS, d), F32).at[:b].set(c).at[b].set(c_ctx)
    mods = _ada_modulation(cond, w_ada, b_ada)

    def mod_chunks(layer, rows):
        m = mods[layer, rows].reshape(-1, 6, d)
        return [m[:, k][:, None, :] for k in range(6)]

    row = lambda v: v.reshape(1, d)

    sh1, sc1, g1, sh2, sc2, g2 = mod_chunks(0, slice(0, b))
    csh1, csc1 = [jnp.broadcast_to(m, (b, 1, d)) for m in mod_chunks(0, slice(b, b + 1))[:2]]
    nsh1, nsc1, ng1, nsh2, nsc2, ng2 = mod_chunks(1, slice(0, b))

    h = _modulate(x, sc1, sh1)
    hc = _modulate(ctx, csc1, csh1)
    wqk = _pair_halves(w_qkv[0][:, :q_dim + kv_dim]).astype(BF16)
    wv = w_qkv[0][:, q_dim + kv_dim:].astype(BF16)
    gain = _pair_halves(jnp.concatenate([jnp.tile(q_gain[0], n_heads), jnp.tile(k_gain[0], n_kv)])).reshape(1, -1)
    scale = jnp.concatenate([jnp.full((q_dim,), ATTN_SCALE * LOG2_E, F32), jnp.ones((kv_dim,), F32)]).reshape(1, -1)
    cos, sin = _rope_tables(s)
    h2d, hc2d = h.reshape(n, d), hc.reshape(b * cl, d)
    qk = _qk_project(h2d, wqk, gain, scale, cos, sin, col_off_cols=0, n_cols=q_dim + kv_dim)
    v = _matmul(h2d, wv, tm=1024, out_dtype=BF16)
    one_c, zero_c = jnp.ones((cl, HEAD_DIM), F32), jnp.zeros((cl, HEAD_DIM), F32)
    kc = _qk_project(hc2d, wqk, gain, scale, one_c, zero_c, col_off_cols=q_dim, n_cols=kv_dim, tm=cl)
    vc = _matmul(hc2d, wv, tm=cl, out_dtype=BF16)
    qk = qk.reshape(b, s, -1)
    k_all = jnp.concatenate([qk[:, :, q_dim:], kc.reshape(b, cl, kv_dim)], axis=1)
    v_all = jnp.concatenate([v.reshape(b, s, n_kv, HEAD_DIM), vc.reshape(b, cl, n_kv, HEAD_DIM)], axis=1)
    sum_cols = jnp.zeros((b, s + cl, n_kv, V_EXTRA), BF16).at[..., 0].set(1.0)
    vt_all = jnp.concatenate([v_all, sum_cols], axis=-1).transpose(0, 2, 3, 1).reshape(b, n_kv * V_ROWS, s + cl)
    o = _attention(qk, k_all, vt_all, n_heads=n_heads, n_kv=n_kv)
    y = _matmul(o.reshape(n, d), w_o[0].astype(BF16))
    x, hn = _moe_layer(x, y, g1, row(ln_g[0, 0]), row(ln_b[0, 0]), sc2, sh2, g2, row(ln_g[0, 1]), row(ln_b[0, 1]),
                       (nsc1, nsh1), w_rg[0], b_rg[0], w_re[0], b_re[0],
                       0, (w_gate, w_up, w_down))

    u = _glu_matmul(hn.reshape(n, d), w_pw1[0].astype(BF16), b_pw1[0])
    cv = _conv_ln_swish(u, s, w_dw[0], b_dw[0], conv_ln_g[0], conv_ln_b[0])
    y = _matmul(cv, w_pw2[0].astype(BF16))
    (x,) = _moe_layer(x, y, ng1, row(ln_g[1, 0]), row(ln_b[1, 0]), nsc2, nsh2, ng2, row(ln_g[1, 1]), row(ln_b[1, 1]),
                      None, w_rg[1], b_rg[1], w_re[1], b_re[1], 1, (w_gate, w_up, w_down))
    return x
```
